```python
import jax
import jax.numpy as jnp
from jax import lax
import numpy as np

D_MODEL = 1024
BATCH = 8
SEQ = 2048
DEPTH = 2

GRID_W = 64
CTX_LEN = 256
N_MIXERS = 2
N_CONV_LAYERS = (DEPTH + 1) // 2
N_NA_LAYERS = DEPTH // 2
CONV_WIDTH = 31
NA_HEADS = 16
NA_HEAD_DIM = D_MODEL // NA_HEADS
NA_KH = 8
NA_KW = 16
NA_KB_W = 2 * NA_KW
NA_NCB = GRID_W // NA_KW
D_FF = ((8 * D_MODEL // 3 + 127) // 128) * 128
FFN_CONV_WIDTH = 3
N_MOD = 6
RMS_EPS = 1e-6
LN_EPS = 1e-5
NEG_INF = -1e30

kernel_name = 'hybrid_conv_natten_dit_block'


def rms_norm(x, g):
    xf = x.astype(jnp.float32)
    y = xf * lax.rsqrt(jnp.mean(xf * xf, axis=-1, keepdims=True) + RMS_EPS)
    return (y * g.astype(jnp.float32)).astype(x.dtype)


def layer_norm(x, g, b):
    xf = x.astype(jnp.float32)
    xc = xf - jnp.mean(xf, axis=-1, keepdims=True)
    y = xc * lax.rsqrt(jnp.mean(xc * xc, axis=-1, keepdims=True) + LN_EPS)
    return (y * g.astype(jnp.float32) + b.astype(jnp.float32)).astype(x.dtype)


def depthwise_conv(x, w):
    width = w.shape[0]
    pad = width // 2
    return lax.conv_general_dilated(
        x, w[:, None, :].astype(x.dtype), window_strides=(1,),
        padding=[(pad, width - 1 - pad)], dimension_numbers=('NWC', 'WIO', 'NWC'),
        feature_group_count=x.shape[-1])


def ada_mod(cond, w, b):
    m = jax.nn.silu(cond) @ w + b
    return jnp.split(m[..., None, :], N_MOD, axis=-1)


def modulate(h, shift, scale):
    return h * (1 + scale) + shift


def conformer_conv(h, w_pw1, b_pw1, w_dw, b_dw, ln_g, ln_b, w_pw2, b_pw2):
    u = h @ w_pw1 + b_pw1
    a, g = jnp.split(u, 2, axis=-1)
    u = a * jax.nn.sigmoid(g)
    u = depthwise_conv(u, w_dw) + b_dw
    u = jax.nn.silu(layer_norm(u, ln_g, ln_b))
    return u @ w_pw2 + b_pw2


def conv_ffn(h, w_up, w_dw, b_dw, w_down):
    u = h @ w_up
    gate, val = jnp.split(u, 2, axis=-1)
    gate = depthwise_conv(gate, w_dw) + b_dw
    return (jax.nn.gelu(gate) * val) @ w_down


def na_column_tables():
    qcol = np.arange(GRID_W).reshape(NA_NCB, NA_KW)
    kstart = np.clip(np.arange(NA_NCB) * NA_KW - NA_KW // 2, 0, GRID_W - NA_KB_W)
    kcol = kstart[:, None] + np.arange(NA_KB_W)[None, :]
    c0 = np.clip(qcol - NA_KW // 2, 0, GRID_W - NA_KW)
    kc = kcol[:, None, :]
    in_win = (kc >= c0[:, :, None]) & (kc < c0[:, :, None] + NA_KW)
    rel_idx = np.clip(kc - qcol[:, :, None] + NA_KW - 1, 0, 2 * NA_KW - 2)
    return kcol, in_win, rel_idx


def na_mixer(a_lat, a_ctx, w_qkv, w_o, rpb, with_ctx_queries):
    b, n, d = a_lat.shape
    rows = n // GRID_W
    kh = min(NA_KH, rows)
    scale = NA_HEAD_DIM ** -0.5
    qkv = (a_lat @ w_qkv).reshape(b, rows, GRID_W, 3, NA_HEADS, NA_HEAD_DIM)
    q_grid, k_grid, v_grid = qkv[..., 0, :, :], qkv[..., 1, :, :], qkv[..., 2, :, :]
    kv_c = (a_ctx @ w_qkv[:, d:]).reshape(b, a_ctx.shape[1], 2, NA_HEADS, NA_HEAD_DIM)
    k_c, v_c = kv_c[..., 0, :, :], kv_c[..., 1, :, :]
    kcol, in_win, rel_idx = na_column_tables()
    rpb_cols = jnp.where(jnp.asarray(in_win)[None, None],
                         rpb.astype(jnp.float32)[:, :, rel_idx], NEG_INF)
    n_loc = kh * NA_KB_W

    def attend_row(r):
        r0 = jnp.clip(r - kh // 2, 0, rows - kh)
        k_slab = lax.dynamic_slice_in_dim(k_grid, r0, kh, axis=1)[:, :, kcol]
        v_slab = lax.dynamic_slice_in_dim(v_grid, r0, kh, axis=1)[:, :, kcol]
        q_row = lax.dynamic_index_in_dim(q_grid, r, axis=1, keepdims=False).reshape(
            b, NA_NCB, NA_KW, NA_HEADS, NA_HEAD_DIM)
        row_rel = r0 + jnp.arange(kh) - r + (NA_KH - 1)
        bias = jnp.transpose(jnp.take(rpb_cols, row_rel, axis=1), (0, 2, 3, 1, 4))
        s_loc = jnp.einsum('bnqhd,bknjhd->bhnqkj', q_row, k_slab,
                           preferred_element_type=jnp.float32) * scale + bias
        s_ctx = jnp.einsum('bnqhd,bchd->bhnqc', q_row, k_c,
                           preferred_element_type=jnp.float32) * scale
        s = jnp.concatenate([s_loc.reshape(b, NA_HEADS, NA_NCB, NA_KW, n_loc), s_ctx], axis=-1)
        p = jax.nn.softmax(s, axis=-1).astype(v_grid.dtype)
        p_loc = p[..., :n_loc].reshape(b, NA_HEADS, NA_NCB, NA_KW, kh, NA_KB_W)
        o = (jnp.einsum('bhnqkj,bknjhd->bnqhd', p_loc, v_slab)
             + jnp.einsum('bhnqc,bchd->bnqhd', p[..., n_loc:], v_c))
        return o.reshape(b, GRID_W, d)

    o_lat = lax.map(attend_row, jnp.arange(rows))
    y_lat = jnp.moveaxis(o_lat, 0, 1).reshape(b, n, d) @ w_o
    y_ctx = None
    if with_ctx_queries:
        q_c = (a_ctx @ w_qkv[:, :d]).reshape(b, a_ctx.shape[1], NA_HEADS, NA_HEAD_DIM)
        s = jnp.einsum('bqhd,bkhd->bhqk', q_c, k_c, preferred_element_type=jnp.float32) * scale
        p = jax.nn.softmax(s, axis=-1).astype(v_c.dtype)
        y_ctx = jnp.einsum('bhqk,bkhd->bqhd', p, v_c).reshape(b, a_ctx.shape[1], d) @ w_o
    return y_lat, y_ctx


def setup_inputs(seed: int = 0) -> dict:
    key = jax.random.key(seed)
    ks = iter(jax.random.split(key, 32))
    D, F = D_MODEL, D_FF

    def nrm(shape, s):
        return jax.random.normal(next(ks), shape, jnp.float32) * s

    def gain(shape):
        return 1.0 + nrm(shape, 0.05)

    return {
        'x': nrm((BATCH, SEQ, D), 1.0),
        'c': nrm((BATCH, D), 1.0),
        'ctx': nrm((BATCH, CTX_LEN, D), 1.0),
        'c_ctx': nrm((D,), 1.0),
        'mod_w': nrm((DEPTH, D, N_MOD * D), 0.5 * D ** -0.5),
        'mod_b': nrm((DEPTH, N_MOD * D), 0.02),
        'mix_pre_g': gain((DEPTH, D)),
        'mix_post_g': gain((DEPTH, D)),
        'ffn_pre_g': gain((DEPTH, D)),
        'ffn_post_g': gain((DEPTH, D)),
        'cv_w_pw1': nrm((N_CONV_LAYERS, D, 2 * D), D ** -0.5),
        'cv_b_pw1': nrm((N_CONV_LAYERS, 2 * D), 0.02),
        'cv_w_dw': nrm((N_CONV_LAYERS, CONV_WIDTH, D), CONV_WIDTH ** -0.5),
        'cv_b_dw': nrm((N_CONV_LAYERS, D), 0.02),
        'cv_ln_g': gain((N_CONV_LAYERS, D)),
        'cv_ln_b': nrm((N_CONV_LAYERS, D), 0.02),
        'cv_w_pw2': nrm((N_CONV_LAYERS, D, D), D ** -0.5),
        'cv_b_pw2': nrm((N_CONV_LAYERS, D), 0.02),
        'na_w_qkv': nrm((N_NA_LAYERS, D, 3 * D), D ** -0.5),
        'na_w_o': nrm((N_NA_LAYERS, D, D), D ** -0.5),
        'na_rpb': nrm((N_NA_LAYERS, NA_HEADS, 2 * NA_KH - 1, 2 * NA_KW - 1), 0.5),
        'ffn_w_up': nrm((DEPTH, D, 2 * F), D ** -0.5),
        'ffn_w_dw': nrm((DEPTH, FFN_CONV_WIDTH, F), FFN_CONV_WIDTH ** -0.5),
        'ffn_b_dw': nrm((DEPTH, F), 0.02),
        'ffn_w_down': nrm((DEPTH, F, D), F ** -0.5),
    }


def reference(x, c, ctx, c_ctx, mod_w, mod_b, mix_pre_g, mix_post_g, ffn_pre_g, ffn_post_g,
              cv_w_pw1, cv_b_pw1, cv_w_dw, cv_b_dw, cv_ln_g, cv_ln_b, cv_w_pw2, cv_b_pw2,
              na_w_qkv, na_w_o, na_rpb, ffn_w_up, ffn_w_dw, ffn_b_dw, ffn_w_down):
    h_ctx = ctx
    for i in range(DEPTH):
        last = i == DEPTH - 1
        j = i // N_MIXERS
        use_na = (i % N_MIXERS) == 1
        sh1, sc1, g1, sh2, sc2, g2 = ada_mod(c, mod_w[i], mod_b[i])
        csh1, csc1, cg1, csh2, csc2, cg2 = ada_mod(c_ctx, mod_w[i], mod_b[i])
        a_lat = modulate(rms_norm(x, mix_pre_g[i]), sh1, sc1)
        need_ctx = use_na or not last
        a_ctx = modulate(rms_norm(h_ctx, mix_pre_g[i]), csh1, csc1) if need_ctx else None
        if use_na:
            y_lat, y_ctx = na_mixer(a_lat, a_ctx, na_w_qkv[j], na_w_o[j], na_rpb[j], not last)
        else:
            cv = (cv_w_pw1[j], cv_b_pw1[j], cv_w_dw[j], cv_b_dw[j],
                  cv_ln_g[j], cv_ln_b[j], cv_w_pw2[j], cv_b_pw2[j])
            y_lat = conformer_conv(a_lat, *cv)
            y_ctx = None if last else conformer_conv(a_ctx, *cv)
        x = x + g1 * rms_norm(y_lat, mix_post_g[i])
        ffn = (ffn_w_up[i], ffn_w_dw[i], ffn_b_dw[i], ffn_w_down[i])
        f_lat = conv_ffn(modulate(rms_norm(x, ffn_pre_g[i]), sh2, sc2), *ffn)
        x = x + g2 * rms_norm(f_lat, ffn_post_g[i])
        if not last:
            h_ctx = h_ctx + cg1 * rms_norm(y_ctx, mix_post_g[i])
            f_ctx = conv_ffn(modulate(rms_norm(h_ctx, ffn_pre_g[i]), csh2, csc2), *ffn)
            h_ctx = h_ctx + cg2 * rms_norm(f_ctx, ffn_post_g[i])
    return x
```

```python
import functools
import math

import numpy as np
import jax
import jax.numpy as jnp
from jax import lax
from jax.experimental import pallas as pl
from jax.experimental.pallas import tpu as pltpu

D_MODEL = 1024
DEPTH = 2
GRID_W = 64
CONV_WIDTH = 31
NA_HEADS = 16
NA_HEAD_DIM = D_MODEL // NA_HEADS
NA_KH = 8
NA_KW = 16
D_FF = 2816
FFN_CONV_WIDTH = 3
N_MOD = 6
RMS_EPS = 1e-6
LN_EPS = 1e-5
NEG_INF = -1e30

V7X_LANES = 128
V7X_SUBLANES = 8
V7X_BF16_ROWS = 16
V7X_VMEM_BYTES = 64 * 1024 * 1024

HALO_ROWS = 2 * V7X_SUBLANES
MOD_ROWS = 16
FFN_CHUNK = 256
CONV_ROWS = 64
ATTN_PAIR = 2 * NA_HEAD_DIM

F32 = jnp.float32
BF16 = jnp.bfloat16


def _vmem_limit(nbytes):
    return int(min(V7X_VMEM_BYTES - 6 * 1024 * 1024, max(2 * nbytes, 24 * 1024 * 1024)))


def _nbytes(shape, dtype):
    return math.prod(shape) * jnp.dtype(dtype).itemsize


def _modnorm(xf, gain, shift, scale):
    ms = jnp.mean(xf * xf, axis=-1, keepdims=True)
    return (xf * lax.rsqrt(ms + RMS_EPS)) * (gain * (1.0 + scale)) + shift


def _post_residual(xf, y, post_g, gate):
    ms = jnp.mean(y * y, axis=-1, keepdims=True)
    return xf + gate * ((y * lax.rsqrt(ms + RMS_EPS)) * post_g)


def _gelu_tanh(x):
    c = math.sqrt(2.0 / math.pi)
    return x * (0.5 * (1.0 + jnp.tanh(c * (x + 0.044715 * (x * x * x)))))


def _dwconv_rows(xe3, taps, pad, n, sub):
    out = None
    for r in range(V7X_SUBLANES):
        offs = [d for d in range(-pad, pad + 1) if d % V7X_SUBLANES == r]
        if not offs:
            continue
        rr = xe3 if r == 0 else pltpu.roll(xe3, V7X_SUBLANES - r, axis=1)
        m = n if r == 0 else n + 1
        part = None
        for d in offs:
            a = d // V7X_SUBLANES
            term = taps[d + pad] * rr[2 + a: 2 + a + m]
            part = term if part is None else part + term
        if r != 0:
            part = jnp.where(sub < V7X_SUBLANES - r, part[0:n], part[1:n + 1])
        out = part if out is None else out + part
    return out


def _ada_body(cs_ref, w_ref, b_ref, o_ref):
    s = cs_ref[...]
    s = s * jax.nn.sigmoid(s)
    m = jnp.dot(s.astype(BF16), w_ref[...].astype(BF16), preferred_element_type=F32) + b_ref[...]
    for row in range(MOD_ROWS):
        o_ref[row] = m[row:row + 1, :]


def _pw1_glu_body(x_ref, gain_ref, shift_ref, scale_ref, w_ref, b_ref, o_ref):
    h = _modnorm(x_ref[...], gain_ref[...], shift_ref[...], scale_ref[...]).astype(BF16)
    a = jnp.dot(h, w_ref[:, :D_MODEL], preferred_element_type=F32) + b_ref[:, :D_MODEL]
    g = jnp.dot(h, w_ref[:, D_MODEL:], preferred_element_type=F32) + b_ref[:, D_MODEL:]
    o_ref[...] = a * jax.nn.sigmoid(g)


def _conv_mix_body(up_ref, u_ref, un_ref, x_ref, wdw_ref, bdw_ref, lng_ref, lnb_ref, w2_ref, b2_ref,
                   postg_ref, gate_ref, o_ref, ue_scr, cv_scr, *, tm, n_tiles):
    i = pl.program_id(1)
    ue_scr[0:HALO_ROWS] = jnp.where(i > 0, up_ref[...], 0.0)
    ue_scr[HALO_ROWS:HALO_ROWS + tm] = u_ref[...]
    ue_scr[HALO_ROWS + tm:] = jnp.where(i < n_tiles - 1, un_ref[...], 0.0)

    nv = CONV_ROWS // V7X_SUBLANES
    sub = lax.broadcasted_iota(jnp.int32, (nv, V7X_SUBLANES, V7X_LANES), 1)
    pad = CONV_WIDTH // 2

    def conv_step(ci, carry):
        base = pl.multiple_of(ci * CONV_ROWS, CONV_ROWS)
        for lc in range(0, D_MODEL, V7X_LANES):
            lanes = slice(lc, lc + V7X_LANES)
            xe3 = ue_scr[pl.ds(base, CONV_ROWS + 2 * HALO_ROWS), lanes].reshape(nv + 4, V7X_SUBLANES, V7X_LANES)
            taps = [wdw_ref[k, :, lanes] for k in range(CONV_WIDTH)]
            y = _dwconv_rows(xe3, taps, pad, nv, sub) + bdw_ref[:, lanes]
            cv_scr[pl.ds(base, CONV_ROWS), lanes] = y.reshape(CONV_ROWS, V7X_LANES)
        return carry

    lax.fori_loop(0, tm // CONV_ROWS, conv_step, 0)

    u = cv_scr[...]
    uc = u - jnp.mean(u, axis=-1, keepdims=True)
    ln = (uc * lax.rsqrt(jnp.mean(uc * uc, axis=-1, keepdims=True) + LN_EPS)) * lng_ref[...] + lnb_ref[...]
    act = (ln * jax.nn.sigmoid(ln)).astype(BF16)
    y = jnp.dot(act, w2_ref[...], preferred_element_type=F32) + b2_ref[...]
    o_ref[...] = _post_residual(x_ref[...], y, postg_ref[...], gate_ref[...])


def _ffn_body(xp_ref, x_ref, xn_ref, gain_ref, shift_ref, scale_ref, wg_ref, wv_ref, wdw_ref, bdw_ref, wd_ref,
              postg_ref, gate_ref, o_ref, he_scr, acc_scr, *, tm, n_tiles, n_chunks):
    i = pl.program_id(1)
    gain, shift, scale = gain_ref[...], shift_ref[...], scale_ref[...]
    x = x_ref[...]
    hp = jnp.where(i > 0, _modnorm(xp_ref[...], gain, shift, scale), 0.0)
    hn = jnp.where(i < n_tiles - 1, _modnorm(xn_ref[...], gain, shift, scale), 0.0)
    he_scr[0:HALO_ROWS] = hp.astype(BF16)
    he_scr[HALO_ROWS:HALO_ROWS + tm] = _modnorm(x, gain, shift, scale).astype(BF16)
    he_scr[HALO_ROWS + tm:] = hn.astype(BF16)
    acc_scr[...] = jnp.zeros_like(acc_scr)

    n = tm // V7X_SUBLANES
    sub = lax.broadcasted_iota(jnp.int32, (n, V7X_SUBLANES, FFN_CHUNK), 1)

    def chunk_step(j, carry):
        ge = jnp.dot(he_scr[...], wg_ref[j], preferred_element_type=F32)
        val = jnp.dot(he_scr[HALO_ROWS:HALO_ROWS + tm], wv_ref[j], preferred_element_type=F32)
        ge3 = ge.reshape(n + 4, V7X_SUBLANES, FFN_CHUNK)
        taps = [wdw_ref[j, k] for k in range(FFN_CONV_WIDTH)]
        gc = _dwconv_rows(ge3, taps, FFN_CONV_WIDTH // 2, n, sub) + bdw_ref[j]
        act = (_gelu_tanh(gc) * val.reshape(n, V7X_SUBLANES, FFN_CHUNK)).reshape(tm, FFN_CHUNK).astype(BF16)
        acc_scr[...] += jnp.dot(act, wd_ref[j], preferred_element_type=F32)
        return carry

    lax.fori_loop(0, n_chunks, chunk_step, 0)
    o_ref[...] = _post_residual(x, acc_scr[...], postg_ref[...], gate_ref[...])


def _proj_body(x_ref, gain_ref, shift_ref, scale_ref, w_ref, *o_refs):
    h = _modnorm(x_ref[...], gain_ref[...], shift_ref[...], scale_ref[...]).astype(BF16)
    for t, o_ref in enumerate(o_refs):
        o_ref[...] = jnp.dot(h, w_ref[:, t * D_MODEL:(t + 1) * D_MODEL], preferred_element_type=F32).astype(o_ref.dtype)


def _out_proj_body(a_ref, x_ref, w_ref, postg_ref, gate_ref, o_ref):
    y = jnp.dot(a_ref[...], w_ref[...], preferred_element_type=F32)
    o_ref[...] = _post_residual(x_ref[...], y, postg_ref[...], gate_ref[...])


def _attn_body(q_ref, k_ref, v_ref, kc_ref, vc_ref, bias_ref, o_ref, *, rows):
    r = pl.program_id(1)
    r0 = jnp.clip(r - NA_KH // 2, 0, rows - NA_KH)
    start = pl.multiple_of(r0 * GRID_W, GRID_W)
    n_loc = NA_KH * GRID_W
    lo = lax.broadcasted_iota(jnp.int32, (GRID_W, ATTN_PAIR), 1) < NA_HEAD_DIM
    nt = (((1,), (1,)), ((), ()))
    for p in range(NA_HEADS // 2):
        lanes = slice(p * ATTN_PAIR, (p + 1) * ATTN_PAIR)
        qp = q_ref[:, lanes]
        zero = jnp.zeros_like(qp)
        qs = jnp.concatenate([jnp.where(lo, qp, zero), jnp.where(lo, zero, qp)], axis=0)
        kp = k_ref[pl.ds(start, n_loc), lanes]
        vp = v_ref[pl.ds(start, n_loc), lanes]
        kcp = kc_ref[:, lanes]
        vcp = vc_ref[:, lanes]
        s_loc = lax.dot_general(qs, kp, nt, preferred_element_type=F32) + bias_ref[p]
        s_ctx = lax.dot_general(qs, kcp, nt, preferred_element_type=F32)
        m = jnp.maximum(jnp.max(s_loc, axis=-1, keepdims=True), jnp.max(s_ctx, axis=-1, keepdims=True))
        p_loc = jnp.exp(s_loc - m).astype(BF16)
        p_ctx = jnp.exp(s_ctx - m).astype(BF16)
        v_ext = jnp.concatenate([vp, jnp.ones_like(vp)], axis=1)
        vc_ext = jnp.concatenate([vcp, jnp.ones_like(vcp)], axis=1)
        o_ext = (jnp.dot(p_loc, v_ext, preferred_element_type=F32)
                 + jnp.dot(p_ctx, vc_ext, preferred_element_type=F32))
        o = o_ext[:, :ATTN_PAIR] / o_ext[:, ATTN_PAIR:ATTN_PAIR + 1]
        o_ref[:, lanes] = jnp.where(lo, o[:GRID_W], o[GRID_W:]).astype(o_ref.dtype)


def _tile_rows(t):
    return 512 if t % 512 == 0 else t


def _row_spec(tm, width=D_MODEL):
    return pl.BlockSpec((None, tm, width), lambda b, i: (b, i, 0))


def _halo_specs(tm, t):
    per = tm // HALO_ROWS
    last = t // HALO_ROWS - 1
    prev = pl.BlockSpec((None, HALO_ROWS, D_MODEL), lambda b, i: (b, jnp.maximum(i * per - 1, 0), 0))
    nxt = pl.BlockSpec((None, HALO_ROWS, D_MODEL), lambda b, i: (b, jnp.minimum((i + 1) * per, last), 0))
    return prev, nxt


def _const_spec(shape):
    nd = len(shape)
    return pl.BlockSpec(shape, lambda b, i: (0,) * nd, pipeline_mode=pl.Buffered(1))


def _vec_spec():
    return pl.BlockSpec((1, D_MODEL), lambda b, i: (0, 0))


def _mod_spec(layer, comp, ctx):
    if ctx is None:
        return pl.BlockSpec((None, None, None, 1, D_MODEL), lambda b, i: (layer, comp, b, 0, 0))
    return pl.BlockSpec((None, None, None, 1, D_MODEL), lambda b, i: (layer, comp, ctx, 0, 0))


def _params(nbytes):
    return pltpu.CompilerParams(dimension_semantics=("arbitrary", "arbitrary"), vmem_limit_bytes=_vmem_limit(nbytes))


def _ada_mod(cs, mod_w, mod_b):
    depth = mod_w.shape[0]
    blk = _nbytes((D_MODEL, D_MODEL), F32)
    return pl.pallas_call(
        _ada_body,
        grid=(depth, N_MOD),
        in_specs=[
            pl.BlockSpec((MOD_ROWS, D_MODEL), lambda l, k: (0, 0)),
            pl.BlockSpec((None, D_MODEL, D_MODEL), lambda l, k: (l, 0, k)),
            pl.BlockSpec((None, 1, D_MODEL), lambda l, k: (l, 0, k)),
        ],
        out_specs=pl.BlockSpec((None, None, MOD_ROWS, 1, D_MODEL), lambda l, k: (l, k, 0, 0, 0)),
        out_shape=jax.ShapeDtypeStruct((depth, N_MOD, MOD_ROWS, 1, D_MODEL), F32),
        compiler_params=pltpu.CompilerParams(dimension_semantics=("arbitrary", "arbitrary"),
                                             vmem_limit_bytes=_vmem_limit(4 * blk)),
        name="ada_mod",
    )(cs, mod_w, mod_b.reshape(depth, 1, N_MOD * D_MODEL))


def _pw1_glu(x, gain, mods, layer, ctx, w_bf, b):
    bsz, t, _ = x.shape
    tm = _tile_rows(t)
    est = 4 * _nbytes((tm, D_MODEL), F32) + _nbytes(w_bf.shape, BF16) + 3 * _nbytes((tm, 2 * D_MODEL), F32)
    return pl.pallas_call(
        _pw1_glu_body,
        grid=(bsz, t // tm),
        in_specs=[_row_spec(tm), _vec_spec(), _mod_spec(layer, 0, ctx), _mod_spec(layer, 1, ctx),
                  _const_spec(w_bf.shape), _const_spec(b.shape)],
        out_specs=_row_spec(tm),
        out_shape=jax.ShapeDtypeStruct((bsz, t, D_MODEL), F32),
        compiler_params=_params(est),
        name="pw1_glu",
    )(x, gain, mods, mods, w_bf, b)


def _conv_mix(u, x, wdw_b, bdw_b, ln_g, ln_b, w2_bf, b2, post_g, mods, layer, ctx):
    bsz, t, _ = x.shape
    tm = _tile_rows(t)
    n_tiles = t // tm
    prev, nxt = _halo_specs(tm, t)
    est = (8 * _nbytes((tm, D_MODEL), F32) + _nbytes(w2_bf.shape, BF16) + _nbytes(wdw_b.shape, F32)
           + 4 * _nbytes((tm, D_MODEL), F32))
    return pl.pallas_call(
        functools.partial(_conv_mix_body, tm=tm, n_tiles=n_tiles),
        grid=(bsz, n_tiles),
        in_specs=[prev, _row_spec(tm), nxt, _row_spec(tm),
                  _const_spec(wdw_b.shape), _const_spec(bdw_b.shape), _vec_spec(), _vec_spec(),
                  _const_spec(w2_bf.shape), _vec_spec(), _vec_spec(), _mod_spec(layer, 2, ctx)],
        out_specs=_row_spec(tm),
        out_shape=jax.ShapeDtypeStruct((bsz, t, D_MODEL), F32),
        scratch_shapes=[pltpu.VMEM((tm + 2 * HALO_ROWS, D_MODEL), F32), pltpu.VMEM((tm, D_MODEL), F32)],
        compiler_params=_params(est),
        name="conv_mix",
    )(u, u, u, x, wdw_b, bdw_b, ln_g, ln_b, w2_bf, b2, post_g, mods)


def _ffn(x, gain, post_g, mods, layer, ctx, wg, wv, wdw_b, bdw_b, wd):
    bsz, t, _ = x.shape
    tm = _tile_rows(t)
    n_tiles = t // tm
    n_chunks = wg.shape[0]
    prev, nxt = _halo_specs(tm, t)
    est = (6 * _nbytes((tm, D_MODEL), F32) + _nbytes(wg.shape, BF16) + _nbytes(wv.shape, BF16)
           + _nbytes(wd.shape, BF16) + 8 * _nbytes((tm + 2 * HALO_ROWS, FFN_CHUNK), F32))
    return pl.pallas_call(
        functools.partial(_ffn_body, tm=tm, n_tiles=n_tiles, n_chunks=n_chunks),
        grid=(bsz, n_tiles),
        in_specs=[prev, _row_spec(tm), nxt, _vec_spec(), _mod_spec(layer, 3, ctx), _mod_spec(layer, 4, ctx),
                  _const_spec(wg.shape), _const_spec(wv.shape), _const_spec(wdw_b.shape), _const_spec(bdw_b.shape),
                  _const_spec(wd.shape), _vec_spec(), _mod_spec(layer, 5, ctx)],
        out_specs=_row_spec(tm),
        out_shape=jax.ShapeDtypeStruct((bsz, t, D_MODEL), F32),
        scratch_shapes=[pltpu.VMEM((tm + 2 * HALO_ROWS, D_MODEL), BF16), pltpu.VMEM((tm, D_MODEL), F32)],
        compiler_params=_params(est),
        name="conv_ffn",
    )(x, x, x, gain, mods, mods, wg, wv, wdw_b, bdw_b, wd, post_g, mods)


def _proj(x, gain, mods, layer, ctx, w_bf, n_out):
    bsz, t, _ = x.shape
    tm = _tile_rows(t)
    est = 2 * _nbytes((tm, D_MODEL), F32) + _nbytes(w_bf.shape, BF16) + (2 * n_out + 4) * _nbytes((tm, D_MODEL), F32)
    return pl.pallas_call(
        _proj_body,
        grid=(bsz, t // tm),
        in_specs=[_row_spec(tm), _vec_spec(), _mod_spec(layer, 0, ctx), _mod_spec(layer, 1, ctx),
                  _const_spec(w_bf.shape)],
        out_specs=[_row_spec(tm)] * n_out,
        out_shape=[jax.ShapeDtypeStruct((bsz, t, D_MODEL), BF16)] * n_out,
        compiler_params=_params(est),
        name="norm_proj",
    )(x, gain, mods, mods, w_bf)


def _out_proj(a, x, w_bf, post_g, mods, layer):
    bsz, t, _ = x.shape
    tm = _tile_rows(t)
    est = 6 * _nbytes((tm, D_MODEL), F32) + _nbytes(w_bf.shape, BF16)
    return pl.pallas_call(
        _out_proj_body,
        grid=(bsz, t // tm),
        in_specs=[_row_spec(tm), _row_spec(tm), _const_spec(w_bf.shape), _vec_spec(), _mod_spec(layer, 2, None)],
        out_specs=_row_spec(tm),
        out_shape=jax.ShapeDtypeStruct((bsz, t, D_MODEL), F32),
        compiler_params=_params(est),
        name="out_proj",
    )(a, x, w_bf, post_g, mods)


def _attention(q, k, v, kc, vc, bias):
    bsz, t, _ = q.shape
    rows = t // GRID_W
    c_len = kc.shape[1]
    kh = NA_KH

    def bias_idx(b, r):
        return (jnp.clip(r - kh // 2, 0, rows - kh) - r + (kh - 1), 0, 0, 0)

    est = (4 * _nbytes((t, D_MODEL), BF16) + 4 * _nbytes((c_len, D_MODEL), BF16)
           + 2 * _nbytes(bias.shape[1:], F32) + 4 * _nbytes((GRID_W, D_MODEL), BF16))
    return pl.pallas_call(
        functools.partial(_attn_body, rows=rows),
        grid=(bsz, rows),
        in_specs=[
            pl.BlockSpec((None, GRID_W, D_MODEL), lambda b, r: (b, r, 0)),
            pl.BlockSpec((None, t, D_MODEL), lambda b, r: (b, 0, 0)),
            pl.BlockSpec((None, t, D_MODEL), lambda b, r: (b, 0, 0)),
            pl.BlockSpec((None, c_len, D_MODEL), lambda b, r: (b, 0, 0)),
            pl.BlockSpec((None, c_len, D_MODEL), lambda b, r: (b, 0, 0)),
            pl.BlockSpec((None,) + bias.shape[1:], bias_idx),
        ],
        out_specs=pl.BlockSpec((None, GRID_W, D_MODEL), lambda b, r: (b, r, 0)),
        out_shape=jax.ShapeDtypeStruct((bsz, t, D_MODEL), BF16),
        compiler_params=_params(est),
        name="na_attention",
    )(q, k, v, kc, vc, bias)


def _attn_bias_table(rpb):
    qc = np.arange(GRID_W)[:, None]
    kc = np.arange(GRID_W)[None, :]
    c0 = np.clip(qc - NA_KW // 2, 0, GRID_W - NA_KW)
    in_win = (kc >= c0) & (kc < c0 + NA_KW)
    rel = np.clip(kc - qc + NA_KW - 1, 0, 2 * NA_KW - 2)
    cols = jnp.where(jnp.asarray(in_win)[None, None], rpb.astype(F32)[:, :, rel], NEG_INF)
    variants = []
    for o in range(NA_KH):
        slab = cols[:, o:o + NA_KH]
        slab = jnp.transpose(slab, (0, 2, 1, 3)).reshape(NA_HEADS // 2, 2 * GRID_W, NA_KH * GRID_W)
        variants.append(slab)
    return jnp.stack(variants, axis=0)


def _ffn_weights(w_up, w_dw, b_dw, w_down):
    nc = D_FF // FFN_CHUNK
    wg = jnp.transpose(w_up[:, :D_FF].astype(BF16).reshape(D_MODEL, nc, FFN_CHUNK), (1, 0, 2))
    wv = jnp.transpose(w_up[:, D_FF:].astype(BF16).reshape(D_MODEL, nc, FFN_CHUNK), (1, 0, 2))
    wd = w_down.astype(BF16).reshape(nc, FFN_CHUNK, D_MODEL)
    wdw_b = jnp.transpose(jnp.broadcast_to(w_dw[:, None, :], (FFN_CONV_WIDTH, V7X_SUBLANES, D_FF))
                          .reshape(FFN_CONV_WIDTH, V7X_SUBLANES, nc, FFN_CHUNK), (2, 0, 1, 3))
    bdw_b = jnp.transpose(jnp.broadcast_to(b_dw[None, :], (V7X_SUBLANES, D_FF))
                          .reshape(V7X_SUBLANES, nc, FFN_CHUNK), (1, 0, 2))
    return wg, wv, wdw_b, bdw_b, wd


def kernel(x, c, ctx, c_ctx, mod_w, mod_b, mix_pre_g, mix_post_g, ffn_pre_g, ffn_post_g, cv_w_pw1, cv_b_pw1, cv_w_dw, cv_b_dw, cv_ln_g, cv_ln_b, cv_w_pw2, cv_b_pw2, na_w_qkv, na_w_o, na_rpb, ffn_w_up, ffn_w_dw, ffn_b_dw, ffn_w_down):
    bsz = x.shape[0]
    assert bsz + 1 <= MOD_ROWS and x.shape[2] == D_MODEL and mod_w.shape[0] == DEPTH
    ctx_row = bsz
    cs = jnp.concatenate([c, c_ctx[None, :], jnp.zeros((MOD_ROWS - bsz - 1, D_MODEL), F32)], axis=0)
    mods = _ada_mod(cs, mod_w, mod_b)

    h_ctx = ctx
    for i in range(DEPTH):
        last = i == DEPTH - 1
        j = i // 2
        use_na = (i % 2) == 1
        pre_g = mix_pre_g[i][None, :]
        post_g = mix_post_g[i][None, :]
        ffn = _ffn_weights(ffn_w_up[i], ffn_w_dw[i], ffn_b_dw[i], ffn_w_down[i])
        if use_na:
            scale = NA_HEAD_DIM ** -0.5
            col_scale = jnp.concatenate([jnp.full((D_MODEL,), scale, F32), jnp.ones((2 * D_MODEL,), F32)])
            w_qkv = (na_w_qkv[j] * col_scale[None, :]).astype(BF16)
            q, k, v = _proj(x, pre_g, mods, i, None, w_qkv, 3)
            kc, vc = _proj(h_ctx, pre_g, mods, i, ctx_row, w_qkv[:, D_MODEL:], 2)
            attn = _attention(q, k, v, kc, vc, _attn_bias_table(na_rpb[j]))
            x = _out_proj(attn, x, na_w_o[j].astype(BF16), post_g, mods, i)
            assert last, "context queries are only needed when another layer follows"
        else:
            w1 = cv_w_pw1[j].astype(BF16)
            b1 = cv_b_pw1[j][None, :]
            wdw_b = jnp.broadcast_to(cv_w_dw[j][:, None, :], (CONV_WIDTH, V7X_SUBLANES, D_MODEL))
            bdw_b = jnp.broadcast_to(cv_b_dw[j][None, :], (V7X_SUBLANES, D_MODEL))
            w2 = cv_w_pw2[j].astype(BF16)
            cv = (wdw_b, bdw_b, cv_ln_g[j][None, :], cv_ln_b[j][None, :], w2, cv_b_pw2[j][None, :], post_g)
            u = _pw1_glu(x, pre_g, mods, i, None, w1, b1)
            x = _conv_mix(u, x, *cv, mods, i, None)
            if not last:
                u_ctx = _pw1_glu(h_ctx, pre_g, mods, i, ctx_row, w1, b1)
                h_ctx = _conv_mix(u_ctx, h_ctx, *cv, mods, i, ctx_row)
        x = _ffn(x, ffn_pre_g[i][None, :], ffn_post_g[i][None, :], mods, i, None, *ffn)
        if not last:
            h_ctx = _ffn(h_ctx, ffn_pre_g[i][None, :], ffn_post_g[i][None, :], mods, i, ctx_row, *ffn)
    return x
```

```python
import functools
import math

import numpy as np
import jax
import jax.numpy as jnp
from jax import lax
from jax.experimental import pallas as pl
from jax.experimental.pallas import tpu as pltpu

D_MODEL = 1024
DEPTH = 2
GRID_W = 64
CONV_WIDTH = 31
NA_HEADS = 16
NA_HEAD_DIM = D_MODEL // NA_HEADS
NA_KH = 8
NA_KW = 16
D_FF = 2816
FFN_CONV_WIDTH = 3
N_MOD = 6
RMS_EPS = 1e-6
LN_EPS = 1e-5
NEG_INF = -1e30

V7X_LANES = 128
V7X_SUBLANES = 8
V7X_BF16_ROWS = 16
V7X_VMEM_BYTES = 64 * 1024 * 1024

HALO_ROWS = 2 * V7X_SUBLANES
MOD_ROWS = 16
FFN_CHUNK = 256
CONV_ROWS = 64
ATTN_PAIR = 2 * NA_HEAD_DIM
ATTN_ROWS_PER_STEP = 4

F32 = jnp.float32
BF16 = jnp.bfloat16


def _vmem_limit(nbytes):
    return int(min(V7X_VMEM_BYTES - 6 * 1024 * 1024, max(2 * nbytes, 24 * 1024 * 1024)))


def _nbytes(shape, dtype):
    return math.prod(shape) * jnp.dtype(dtype).itemsize


def _modnorm(xf, gain, shift, scale):
    ms = jnp.mean(xf * xf, axis=-1, keepdims=True)
    return (xf * lax.rsqrt(ms + RMS_EPS)) * (gain * (1.0 + scale)) + shift


def _post_residual(xf, y, post_g, gate):
    ms = jnp.mean(y * y, axis=-1, keepdims=True)
    return xf + gate * ((y * lax.rsqrt(ms + RMS_EPS)) * post_g)


def _gelu_tanh(x):
    c = math.sqrt(2.0 / math.pi)
    return x * (0.5 * (1.0 + jnp.tanh(c * (x + 0.044715 * (x * x * x)))))


def _dwconv_rows(xe3, taps, pad, n, sub):
    out = None
    for r in range(V7X_SUBLANES):
        offs = [d for d in range(-pad, pad + 1) if d % V7X_SUBLANES == r]
        if not offs:
            continue
        rr = xe3 if r == 0 else pltpu.roll(xe3, V7X_SUBLANES - r, axis=1)
        m = n if r == 0 else n + 1
        part = None
        for d in offs:
            a = d // V7X_SUBLANES
            term = taps[d + pad] * rr[2 + a: 2 + a + m]
            part = term if part is None else part + term
        if r != 0:
            part = jnp.where(sub < V7X_SUBLANES - r, part[0:n], part[1:n + 1])
        out = part if out is None else out + part
    return out


def _ada_body(cs_ref, w_ref, b_ref, o_ref):
    s = cs_ref[...]
    s = s * jax.nn.sigmoid(s)
    m = jnp.dot(s.astype(BF16), w_ref[...].astype(BF16), preferred_element_type=F32) + b_ref[...]
    for row in range(MOD_ROWS):
        o_ref[row] = m[row:row + 1, :]


def _pw1_glu_body(x_ref, gain_ref, shift_ref, scale_ref, w_ref, b_ref, o_ref):
    h = _modnorm(x_ref[...], gain_ref[...], shift_ref[...], scale_ref[...]).astype(BF16)
    a = jnp.dot(h, w_ref[:, :D_MODEL], preferred_element_type=F32) + b_ref[:, :D_MODEL]
    g = jnp.dot(h, w_ref[:, D_MODEL:], preferred_element_type=F32) + b_ref[:, D_MODEL:]
    o_ref[...] = a * jax.nn.sigmoid(g)


def _conv_mix_body(up_ref, u_ref, un_ref, x_ref, wdw_ref, bdw_ref, lng_ref, lnb_ref, w2_ref, b2_ref,
                   postg_ref, gate_ref, o_ref, ue_scr, cv_scr, *, tm, n_tiles):
    i = pl.program_id(1)
    ue_scr[0:HALO_ROWS] = jnp.where(i > 0, up_ref[...], 0.0)
    ue_scr[HALO_ROWS:HALO_ROWS + tm] = u_ref[...]
    ue_scr[HALO_ROWS + tm:] = jnp.where(i < n_tiles - 1, un_ref[...], 0.0)

    nv = CONV_ROWS // V7X_SUBLANES
    sub = lax.broadcasted_iota(jnp.int32, (nv, V7X_SUBLANES, V7X_LANES), 1)
    pad = CONV_WIDTH // 2

    def conv_step(ci, carry):
        base = pl.multiple_of(ci * CONV_ROWS, CONV_ROWS)
        for lc in range(0, D_MODEL, V7X_LANES):
            lanes = slice(lc, lc + V7X_LANES)
            xe3 = ue_scr[pl.ds(base, CONV_ROWS + 2 * HALO_ROWS), lanes].reshape(nv + 4, V7X_SUBLANES, V7X_LANES)
            taps = [wdw_ref[k, :, lanes] for k in range(CONV_WIDTH)]
            y = _dwconv_rows(xe3, taps, pad, nv, sub) + bdw_ref[:, lanes]
            cv_scr[pl.ds(base, CONV_ROWS), lanes] = y.reshape(CONV_ROWS, V7X_LANES)
        return carry

    lax.fori_loop(0, tm // CONV_ROWS, conv_step, 0)

    u = cv_scr[...]
    uc = u - jnp.mean(u, axis=-1, keepdims=True)
    ln = (uc * lax.rsqrt(jnp.mean(uc * uc, axis=-1, keepdims=True) + LN_EPS)) * lng_ref[...] + lnb_ref[...]
    act = (ln * jax.nn.sigmoid(ln)).astype(BF16)
    y = jnp.dot(act, w2_ref[...], preferred_element_type=F32) + b2_ref[...]
    o_ref[...] = _post_residual(x_ref[...], y, postg_ref[...], gate_ref[...])


def _ffn_body(xp_ref, x_ref, xn_ref, gain_ref, shift_ref, scale_ref, wup_ref, wdw_ref, bdw_ref, wd_ref,
              postg_ref, gate_ref, o_ref, *, tm, n_tiles):
    i = pl.program_id(1)
    gain, shift, scale = gain_ref[...], shift_ref[...], scale_ref[...]
    x = x_ref[...]
    hp = jnp.where(i > 0, _modnorm(xp_ref[...], gain, shift, scale), 0.0).astype(BF16)
    hn = jnp.where(i < n_tiles - 1, _modnorm(xn_ref[...], gain, shift, scale), 0.0).astype(BF16)
    h = _modnorm(x, gain, shift, scale).astype(BF16)
    he = jnp.concatenate([hp, h, hn], axis=0)

    n = tm // V7X_SUBLANES
    sub = lax.broadcasted_iota(jnp.int32, (n, V7X_SUBLANES, FFN_CHUNK), 1)
    acc = None
    for j in range(D_FF // FFN_CHUNK):
        cols = slice(j * FFN_CHUNK, (j + 1) * FFN_CHUNK)
        vcols = slice(D_FF + j * FFN_CHUNK, D_FF + (j + 1) * FFN_CHUNK)
        ge = jnp.dot(he, wup_ref[:, cols], preferred_element_type=F32)
        val = jnp.dot(h, wup_ref[:, vcols], preferred_element_type=F32)
        ge3 = ge.reshape(n + 4, V7X_SUBLANES, FFN_CHUNK)
        taps = [wdw_ref[k, :, cols] for k in range(FFN_CONV_WIDTH)]
        gc = _dwconv_rows(ge3, taps, FFN_CONV_WIDTH // 2, n, sub) + bdw_ref[:, cols]
        act = (_gelu_tanh(gc) * val.reshape(n, V7X_SUBLANES, FFN_CHUNK)).reshape(tm, FFN_CHUNK).astype(BF16)
        part = jnp.dot(act, wd_ref[cols, :], preferred_element_type=F32)
        acc = part if acc is None else acc + part
    o_ref[...] = _post_residual(x, acc, postg_ref[...], gate_ref[...])


def _proj_body(x_ref, gain_ref, shift_ref, scale_ref, w_ref, *o_refs, col0):
    h = _modnorm(x_ref[...], gain_ref[...], shift_ref[...], scale_ref[...]).astype(BF16)
    for t, o_ref in enumerate(o_refs):
        cols = slice(col0 + t * D_MODEL, col0 + (t + 1) * D_MODEL)
        o_ref[...] = jnp.dot(h, w_ref[:, cols], preferred_element_type=F32).astype(o_ref.dtype)


def _out_proj_body(a_ref, x_ref, w_ref, postg_ref, gate_ref, o_ref):
    y = jnp.dot(a_ref[...], w_ref[...], preferred_element_type=F32)
    o_ref[...] = _post_residual(x_ref[...], y, postg_ref[...], gate_ref[...])


def _bias_body(src_ref, o_ref):
    qc = lax.broadcasted_iota(jnp.int32, (GRID_W, ATTN_PAIR), 0)
    kc = lax.broadcasted_iota(jnp.int32, (GRID_W, ATTN_PAIR), 1) & (GRID_W - 1)
    c0 = jnp.clip(qc - NA_KW // 2, 0, GRID_W - NA_KW)
    in_win = (kc >= c0) & (kc < c0 + NA_KW)
    base_shift = V7X_LANES - (NA_KW - 1)
    for s in range(2 * NA_KH - 2):
        for e in range(2):
            src = jnp.broadcast_to(src_ref[s, e], (GRID_W, ATTN_PAIR))
            tile = pltpu.roll(src, base_shift, 1, stride=1, stride_axis=0)
            o_ref[s, e * GRID_W:(e + 1) * GRID_W, :] = jnp.where(in_win, tile, NEG_INF)


def _attn_body(q_ref, k_ref, v_ref, kc_ref, vc_ref, bias_ref, o_ref, *, rows, rows_per_step):
    n_loc = NA_KH * GRID_W
    lo = lax.broadcasted_iota(jnp.int32, (GRID_W, ATTN_PAIR), 1) < NA_HEAD_DIM
    nt = (((1,), (1,)), ((), ()))
    for p in range(NA_HEADS // 2):
        lanes = slice(p * ATTN_PAIR, (p + 1) * ATTN_PAIR)
        kcp = kc_ref[:, lanes]
        vcp = vc_ref[:, lanes]
        vc_ext = jnp.concatenate([vcp, jnp.ones_like(vcp)], axis=1)
        for rr in range(rows_per_step):
            r = pl.program_id(1) * rows_per_step + rr
            r0 = jnp.clip(r - NA_KH // 2, 0, rows - NA_KH)
            row_off = r0 - r + (NA_KH - 1)
            start = pl.multiple_of(r0 * GRID_W, GRID_W)
            q_rows = slice(rr * GRID_W, (rr + 1) * GRID_W)
            qp = q_ref[q_rows, lanes]
            zero = jnp.zeros_like(qp)
            qs = jnp.concatenate([jnp.where(lo, qp, zero), jnp.where(lo, zero, qp)], axis=0)
            kp = k_ref[pl.ds(start, n_loc), lanes]
            vp = v_ref[pl.ds(start, n_loc), lanes]
            bias = jnp.concatenate([bias_ref[p, row_off + 2 * jj] for jj in range(NA_KH // 2)], axis=1)
            s_loc = lax.dot_general(qs, kp, nt, preferred_element_type=F32) + bias
            s_ctx = lax.dot_general(qs, kcp, nt, preferred_element_type=F32)
            m = jnp.maximum(jnp.max(s_loc, axis=-1, keepdims=True), jnp.max(s_ctx, axis=-1, keepdims=True))
            p_loc = jnp.exp(s_loc - m).astype(BF16)
            p_ctx = jnp.exp(s_ctx - m).astype(BF16)
            v_ext = jnp.concatenate([vp, jnp.ones_like(vp)], axis=1)
            o_ext = (jnp.dot(p_loc, v_ext, preferred_element_type=F32)
                     + jnp.dot(p_ctx, vc_ext, preferred_element_type=F32))
            o = o_ext[:, :ATTN_PAIR] / o_ext[:, ATTN_PAIR:ATTN_PAIR + 1]
            o_ref[q_rows, lanes] = jnp.where(lo, o[:GRID_W], o[GRID_W:]).astype(o_ref.dtype)


def _tile_rows(t):
    return 512 if t % 512 == 0 else t


def _row_spec(tm, width=D_MODEL):
    return pl.BlockSpec((None, tm, width), lambda b, i: (b, i, 0))


def _halo_specs(tm, t):
    per = tm // HALO_ROWS
    last = t // HALO_ROWS - 1
    prev = pl.BlockSpec((None, HALO_ROWS, D_MODEL), lambda b, i: (b, jnp.maximum(i * per - 1, 0), 0))
    nxt = pl.BlockSpec((None, HALO_ROWS, D_MODEL), lambda b, i: (b, jnp.minimum((i + 1) * per, last), 0))
    return prev, nxt


def _const_spec(shape):
    nd = len(shape)
    return pl.BlockSpec(shape, lambda b, i: (0,) * nd, pipeline_mode=pl.Buffered(1))


def _vec_spec():
    return pl.BlockSpec((1, D_MODEL), lambda b, i: (0, 0))


def _mod_spec(layer, comp, ctx):
    if ctx is None:
        return pl.BlockSpec((None, None, None, 1, D_MODEL), lambda b, i: (layer, comp, b, 0, 0))
    return pl.BlockSpec((None, None, None, 1, D_MODEL), lambda b, i: (layer, comp, ctx, 0, 0))


def _params(nbytes):
    return pltpu.CompilerParams(dimension_semantics=("arbitrary", "arbitrary"), vmem_limit_bytes=_vmem_limit(nbytes))


def _ada_mod(cs, mod_w, mod_b):
    depth = mod_w.shape[0]
    blk = _nbytes((D_MODEL, D_MODEL), F32)
    return pl.pallas_call(
        _ada_body,
        grid=(depth, N_MOD),
        in_specs=[
            pl.BlockSpec((MOD_ROWS, D_MODEL), lambda l, k: (0, 0)),
            pl.BlockSpec((None, D_MODEL, D_MODEL), lambda l, k: (l, 0, k)),
            pl.BlockSpec((None, 1, D_MODEL), lambda l, k: (l, 0, k)),
        ],
        out_specs=pl.BlockSpec((None, None, MOD_ROWS, 1, D_MODEL), lambda l, k: (l, k, 0, 0, 0)),
        out_shape=jax.ShapeDtypeStruct((depth, N_MOD, MOD_ROWS, 1, D_MODEL), F32),
        compiler_params=pltpu.CompilerParams(dimension_semantics=("arbitrary", "arbitrary"),
                                             vmem_limit_bytes=_vmem_limit(4 * blk)),
        name="ada_mod",
    )(cs, mod_w, mod_b.reshape(depth, 1, N_MOD * D_MODEL))


def _pw1_glu(x, gain, mods, layer, ctx, w_bf, b):
    bsz, t, _ = x.shape
    tm = _tile_rows(t)
    est = 4 * _nbytes((tm, D_MODEL), F32) + _nbytes(w_bf.shape, BF16) + 3 * _nbytes((tm, 2 * D_MODEL), F32)
    return pl.pallas_call(
        _pw1_glu_body,
        grid=(bsz, t // tm),
        in_specs=[_row_spec(tm), _vec_spec(), _mod_spec(layer, 0, ctx), _mod_spec(layer, 1, ctx),
                  _const_spec(w_bf.shape), _const_spec(b.shape)],
        out_specs=_row_spec(tm),
        out_shape=jax.ShapeDtypeStruct((bsz, t, D_MODEL), F32),
        compiler_params=_params(est),
        name="pw1_glu",
    )(x, gain, mods, mods, w_bf, b)


def _conv_mix(u, x, wdw_b, bdw_b, ln_g, ln_b, w2_bf, b2, post_g, mods, layer, ctx):
    bsz, t, _ = x.shape
    tm = _tile_rows(t)
    n_tiles = t // tm
    prev, nxt = _halo_specs(tm, t)
    est = (8 * _nbytes((tm, D_MODEL), F32) + _nbytes(w2_bf.shape, BF16) + _nbytes(wdw_b.shape, F32)
           + 4 * _nbytes((tm, D_MODEL), F32))
    return pl.pallas_call(
        functools.partial(_conv_mix_body, tm=tm, n_tiles=n_tiles),
        grid=(bsz, n_tiles),
        in_specs=[prev, _row_spec(tm), nxt, _row_spec(tm),
                  _const_spec(wdw_b.shape), _const_spec(bdw_b.shape), _vec_spec(), _vec_spec(),
                  _const_spec(w2_bf.shape), _vec_spec(), _vec_spec(), _mod_spec(layer, 2, ctx)],
        out_specs=_row_spec(tm),
        out_shape=jax.ShapeDtypeStruct((bsz, t, D_MODEL), F32),
        scratch_shapes=[pltpu.VMEM((tm + 2 * HALO_ROWS, D_MODEL), F32), pltpu.VMEM((tm, D_MODEL), F32)],
        compiler_params=_params(est),
        name="conv_mix",
    )(u, u, u, x, wdw_b, bdw_b, ln_g, ln_b, w2_bf, b2, post_g, mods)


def _ffn(x, gain, post_g, mods, layer, ctx, wup, wdw_b, bdw_b, wd):
    bsz, t, _ = x.shape
    tm = _tile_rows(t)
    n_tiles = t // tm
    prev, nxt = _halo_specs(tm, t)
    est = (8 * _nbytes((tm, D_MODEL), F32) + _nbytes(wup.shape, BF16) + _nbytes(wd.shape, BF16)
           + 8 * _nbytes((tm + 2 * HALO_ROWS, FFN_CHUNK), F32))
    return pl.pallas_call(
        functools.partial(_ffn_body, tm=tm, n_tiles=n_tiles),
        grid=(bsz, n_tiles),
        in_specs=[prev, _row_spec(tm), nxt, _vec_spec(), _mod_spec(layer, 3, ctx), _mod_spec(layer, 4, ctx),
                  _const_spec(wup.shape), _const_spec(wdw_b.shape), _const_spec(bdw_b.shape),
                  _const_spec(wd.shape), _vec_spec(), _mod_spec(layer, 5, ctx)],
        out_specs=_row_spec(tm),
        out_shape=jax.ShapeDtypeStruct((bsz, t, D_MODEL), F32),
        compiler_params=_params(est),
        name="conv_ffn",
    )(x, x, x, gain, mods, mods, wup, wdw_b, bdw_b, wd, post_g, mods)


def _proj(x, gain, mods, layer, ctx, w_bf, col0, n_out):
    bsz, t, _ = x.shape
    tm = _tile_rows(t)
    est = 2 * _nbytes((tm, D_MODEL), F32) + _nbytes(w_bf.shape, BF16) + (2 * n_out + 4) * _nbytes((tm, D_MODEL), F32)
    return pl.pallas_call(
        functools.partial(_proj_body, col0=col0),
        grid=(bsz, t // tm),
        in_specs=[_row_spec(tm), _vec_spec(), _mod_spec(layer, 0, ctx), _mod_spec(layer, 1, ctx),
                  _const_spec(w_bf.shape)],
        out_specs=[_row_spec(tm)] * n_out,
        out_shape=[jax.ShapeDtypeStruct((bsz, t, D_MODEL), BF16)] * n_out,
        compiler_params=_params(est),
        name="norm_proj",
    )(x, gain, mods, mods, w_bf)


def _out_proj(a, x, w_bf, post_g, mods, layer):
    bsz, t, _ = x.shape
    tm = _tile_rows(t)
    est = 6 * _nbytes((tm, D_MODEL), F32) + _nbytes(w_bf.shape, BF16)
    return pl.pallas_call(
        _out_proj_body,
        grid=(bsz, t // tm),
        in_specs=[_row_spec(tm), _row_spec(tm), _const_spec(w_bf.shape), _vec_spec(), _mod_spec(layer, 2, None)],
        out_specs=_row_spec(tm),
        out_shape=jax.ShapeDtypeStruct((bsz, t, D_MODEL), F32),
        compiler_params=_params(est),
        name="out_proj",
    )(a, x, w_bf, post_g, mods)


def _attention(q, k, v, kc, vc, bias):
    bsz, t, _ = q.shape
    rows = t // GRID_W
    c_len = kc.shape[1]
    rps = ATTN_ROWS_PER_STEP
    assert rows % rps == 0
    tq = rps * GRID_W
    est = (4 * _nbytes((t, D_MODEL), BF16) + 4 * _nbytes((c_len, D_MODEL), BF16)
           + _nbytes(bias.shape, F32) + 4 * _nbytes((tq, D_MODEL), BF16))
    return pl.pallas_call(
        functools.partial(_attn_body, rows=rows, rows_per_step=rps),
        grid=(bsz, rows // rps),
        in_specs=[
            pl.BlockSpec((None, tq, D_MODEL), lambda b, r: (b, r, 0)),
            pl.BlockSpec((None, t, D_MODEL), lambda b, r: (b, 0, 0)),
            pl.BlockSpec((None, t, D_MODEL), lambda b, r: (b, 0, 0)),
            pl.BlockSpec((None, c_len, D_MODEL), lambda b, r: (b, 0, 0)),
            pl.BlockSpec((None, c_len, D_MODEL), lambda b, r: (b, 0, 0)),
            _const_spec(bias.shape),
        ],
        out_specs=pl.BlockSpec((None, tq, D_MODEL), lambda b, r: (b, r, 0)),
        out_shape=jax.ShapeDtypeStruct((bsz, t, D_MODEL), BF16),
        compiler_params=_params(est),
        name="na_attention",
    )(q, k, v, kc, vc, bias)


def _attn_bias_table(rpb):
    n_pairs = NA_HEADS // 2
    n_s = 2 * NA_KH - 2
    rp = jnp.pad(rpb.astype(F32), ((0, 0), (0, 0), (0, GRID_W - (2 * NA_KW - 1))))
    src = jnp.concatenate([rp[:, :n_s], rp[:, 1:]], axis=-1)
    src = jnp.transpose(src.reshape(n_pairs, 2, n_s, ATTN_PAIR), (0, 2, 1, 3))[:, :, :, None, :]
    return pl.pallas_call(
        _bias_body,
        grid=(n_pairs,),
        in_specs=[pl.BlockSpec((None, n_s, 2, 1, ATTN_PAIR), lambda p: (p, 0, 0, 0, 0))],
        out_specs=pl.BlockSpec((None, n_s, 2 * GRID_W, ATTN_PAIR), lambda p: (p, 0, 0, 0)),
        out_shape=jax.ShapeDtypeStruct((n_pairs, n_s, 2 * GRID_W, ATTN_PAIR), F32),
        name="na_bias",
    )(src)


def _ffn_weights(w_up, w_dw, b_dw, w_down):
    wdw_b = jnp.broadcast_to(w_dw[:, None, :], (FFN_CONV_WIDTH, V7X_SUBLANES, D_FF))
    bdw_b = jnp.broadcast_to(b_dw[None, :], (V7X_SUBLANES, D_FF))
    return w_up.astype(BF16), wdw_b, bdw_b, w_down.astype(BF16)


def kernel(x, c, ctx, c_ctx, mod_w, mod_b, mix_pre_g, mix_post_g, ffn_pre_g, ffn_post_g, cv_w_pw1, cv_b_pw1, cv_w_dw, cv_b_dw, cv_ln_g, cv_ln_b, cv_w_pw2, cv_b_pw2, na_w_qkv, na_w_o, na_rpb, ffn_w_up, ffn_w_dw, ffn_b_dw, ffn_w_down):
    bsz = x.shape[0]
    assert bsz + 1 <= MOD_ROWS and x.shape[2] == D_MODEL and mod_w.shape[0] == DEPTH
    ctx_row = bsz
    cs = jnp.concatenate([c, c_ctx[None, :], jnp.zeros((MOD_ROWS - bsz - 1, D_MODEL), F32)], axis=0)
    mods = _ada_mod(cs, mod_w, mod_b)

    h_ctx = ctx
    for i in range(DEPTH):
        last = i == DEPTH - 1
        j = i // 2
        use_na = (i % 2) == 1
        pre_g = mix_pre_g[i][None, :]
        post_g = mix_post_g[i][None, :]
        ffn = _ffn_weights(ffn_w_up[i], ffn_w_dw[i], ffn_b_dw[i], ffn_w_down[i])
        if use_na:
            scale = NA_HEAD_DIM ** -0.5
            col_scale = jnp.concatenate([jnp.full((D_MODEL,), scale, F32), jnp.ones((2 * D_MODEL,), F32)])
            w_qkv = (na_w_qkv[j] * col_scale[None, :]).astype(BF16)
            q, k, v = _proj(x, pre_g, mods, i, None, w_qkv, 0, 3)
            kc, vc = _proj(h_ctx, pre_g, mods, i, ctx_row, w_qkv, D_MODEL, 2)
            attn = _attention(q, k, v, kc, vc, _attn_bias_table(na_rpb[j]))
            x = _out_proj(attn, x, na_w_o[j].astype(BF16), post_g, mods, i)
            assert last, "context queries are only needed when another layer follows"
        else:
            w1 = cv_w_pw1[j].astype(BF16)
            b1 = cv_b_pw1[j][None, :]
            wdw_b = jnp.broadcast_to(cv_w_dw[j][:, None, :], (CONV_WIDTH, V7X_SUBLANES, D_MODEL))
            bdw_b = jnp.broadcast_to(cv_b_dw[j][None, :], (V7X_SUBLANES, D_MODEL))
            w2 = cv_w_pw2[j].astype(BF16)
            cv = (wdw_b, bdw_b, cv_ln_g[j][None, :], cv_ln_b[j][None, :], w2, cv_b_pw2[j][None, :], post_g)
            u = _pw1_glu(x, pre_g, mods, i, None, w1, b1)
            x = _conv_mix(u, x, *cv, mods, i, None)
            if not last:
                u_ctx = _pw1_glu(h_ctx, pre_g, mods, i, ctx_row, w1, b1)
                h_ctx = _conv_mix(u_ctx, h_ctx, *cv, mods, i, ctx_row)
        x = _ffn(x, ffn_pre_g[i][None, :], ffn_post_g[i][None, :], mods, i, None, *ffn)
        if not last:
            h_ctx = _ffn(h_ctx, ffn_pre_g[i][None, :], ffn_post_g[i][None, :], mods, i, ctx_row, *ffn)
    return x
```

```python
import functools
import math

import numpy as np
import jax
import jax.numpy as jnp
from jax import lax
from jax.experimental import pallas as pl
from jax.experimental.pallas import tpu as pltpu

D_MODEL = 1024
DEPTH = 2
GRID_W = 64
CONV_WIDTH = 31
NA_HEADS = 16
NA_HEAD_DIM = D_MODEL // NA_HEADS
NA_KH = 8
NA_KW = 16
D_FF = 2816
FFN_CONV_WIDTH = 3
N_MOD = 6
RMS_EPS = 1e-6
LN_EPS = 1e-5
NEG_INF = -1e30

V7X_LANES = 128
V7X_SUBLANES = 8
V7X_BF16_ROWS = 16
V7X_VMEM_BYTES = 64 * 1024 * 1024

HALO_ROWS = 2 * V7X_SUBLANES
MOD_ROWS = 16
FFN_CHUNK = 256
FFN_ROWS = 512
CONV_ROWS = 64
ATTN_PAIR = 2 * NA_HEAD_DIM
ATTN_ROWS_PER_STEP = 4
ATTN_LOOKAHEAD = 3

F32 = jnp.float32
BF16 = jnp.bfloat16


def _vmem_limit(nbytes):
    return int(min(V7X_VMEM_BYTES - 6 * 1024 * 1024, max(2 * nbytes, 24 * 1024 * 1024)))


def _nbytes(shape, dtype):
    return math.prod(shape) * jnp.dtype(dtype).itemsize


def _modnorm(xf, gain, shift, scale):
    ms = jnp.mean(xf * xf, axis=-1, keepdims=True)
    return (xf * lax.rsqrt(ms + RMS_EPS)) * (gain * (1.0 + scale)) + shift


def _post_residual(xf, y, post_g, gate):
    ms = jnp.mean(y * y, axis=-1, keepdims=True)
    return xf + gate * ((y * lax.rsqrt(ms + RMS_EPS)) * post_g)


def _gelu_tanh(x):
    c = math.sqrt(2.0 / math.pi)
    return x * (0.5 * (1.0 + jnp.tanh(c * (x + 0.044715 * (x * x * x)))))


def _dwconv_rows(xe3, taps, pad, n, sub):
    out = None
    for r in range(V7X_SUBLANES):
        offs = [d for d in range(-pad, pad + 1) if d % V7X_SUBLANES == r]
        if not offs:
            continue
        rr = xe3 if r == 0 else pltpu.roll(xe3, V7X_SUBLANES - r, axis=1)
        m = n if r == 0 else n + 1
        part = None
        for d in offs:
            a = d // V7X_SUBLANES
            term = taps[d + pad] * rr[2 + a: 2 + a + m]
            part = term if part is None else part + term
        if r != 0:
            part = jnp.where(sub < V7X_SUBLANES - r, part[0:n], part[1:n + 1])
        out = part if out is None else out + part
    return out


def _ada_body(cs_ref, w_ref, b_ref, o_ref):
    s = cs_ref[...]
    s = s * jax.nn.sigmoid(s)
    m = jnp.dot(s.astype(BF16), w_ref[...].astype(BF16), preferred_element_type=F32) + b_ref[...]
    for row in range(MOD_ROWS):
        o_ref[row] = m[row:row + 1, :]


def _pw1_glu_body(x_ref, gain_ref, shift_ref, scale_ref, w_ref, b_ref, o_ref):
    h = _modnorm(x_ref[...], gain_ref[...], shift_ref[...], scale_ref[...]).astype(BF16)
    a = jnp.dot(h, w_ref[:, :D_MODEL], preferred_element_type=F32) + b_ref[:, :D_MODEL]
    g = jnp.dot(h, w_ref[:, D_MODEL:], preferred_element_type=F32) + b_ref[:, D_MODEL:]
    o_ref[...] = a * jax.nn.sigmoid(g)


def _conv_mix_body(up_ref, u_ref, un_ref, x_ref, wdw_ref, bdw_ref, lng_ref, lnb_ref, w2_ref, b2_ref,
                   postg_ref, gate_ref, o_ref, ue_scr, cv_scr, *, tm, n_tiles):
    i = pl.program_id(1)
    ue_scr[0:HALO_ROWS] = jnp.where(i > 0, up_ref[...], 0.0)
    ue_scr[HALO_ROWS:HALO_ROWS + tm] = u_ref[...]
    ue_scr[HALO_ROWS + tm:] = jnp.where(i < n_tiles - 1, un_ref[...], 0.0)

    nv = CONV_ROWS // V7X_SUBLANES
    sub = lax.broadcasted_iota(jnp.int32, (nv, V7X_SUBLANES, V7X_LANES), 1)
    pad = CONV_WIDTH // 2

    def conv_step(ci, carry):
        base = pl.multiple_of(ci * CONV_ROWS, CONV_ROWS)
        for lc in range(0, D_MODEL, V7X_LANES):
            lanes = slice(lc, lc + V7X_LANES)
            xe3 = ue_scr[pl.ds(base, CONV_ROWS + 2 * HALO_ROWS), lanes].reshape(nv + 4, V7X_SUBLANES, V7X_LANES)
            taps = [wdw_ref[k, :, lanes] for k in range(CONV_WIDTH)]
            y = _dwconv_rows(xe3, taps, pad, nv, sub) + bdw_ref[:, lanes]
            cv_scr[pl.ds(base, CONV_ROWS), lanes] = y.reshape(CONV_ROWS, V7X_LANES)
        return carry

    lax.fori_loop(0, tm // CONV_ROWS, conv_step, 0)

    u = cv_scr[...]
    uc = u - jnp.mean(u, axis=-1, keepdims=True)
    ln = (uc * lax.rsqrt(jnp.mean(uc * uc, axis=-1, keepdims=True) + LN_EPS)) * lng_ref[...] + lnb_ref[...]
    act = (ln * jax.nn.sigmoid(ln)).astype(BF16)
    y = jnp.dot(act, w2_ref[...], preferred_element_type=F32) + b2_ref[...]
    o_ref[...] = _post_residual(x_ref[...], y, postg_ref[...], gate_ref[...])


def _ffn_body(xp_ref, x_ref, xn_ref, gain_ref, shift_ref, scale_ref, wup_ref, wdw_ref, bdw_ref, wd_ref,
              postg_ref, gate_ref, o_ref, *, tm, n_tiles):
    i = pl.program_id(1)
    gain, shift, scale = gain_ref[...], shift_ref[...], scale_ref[...]
    x = x_ref[...]
    hp = jnp.where(i > 0, _modnorm(xp_ref[...], gain, shift, scale), 0.0).astype(BF16)
    hn = jnp.where(i < n_tiles - 1, _modnorm(xn_ref[...], gain, shift, scale), 0.0).astype(BF16)
    h = _modnorm(x, gain, shift, scale).astype(BF16)
    he = jnp.concatenate([hp, h, hn], axis=0)

    n = tm // V7X_SUBLANES
    sub = lax.broadcasted_iota(jnp.int32, (n, V7X_SUBLANES, FFN_CHUNK), 1)
    n_chunks = D_FF // FFN_CHUNK

    def up_proj(j):
        cols = slice(j * FFN_CHUNK, (j + 1) * FFN_CHUNK)
        vcols = slice(D_FF + j * FFN_CHUNK, D_FF + (j + 1) * FFN_CHUNK)
        ge = jnp.dot(he, wup_ref[:, cols], preferred_element_type=F32)
        val = jnp.dot(h, wup_ref[:, vcols], preferred_element_type=F32)
        return ge, val

    acc = None
    nxt = up_proj(0)
    for j in range(n_chunks):
        ge, val = nxt
        if j + 1 < n_chunks:
            nxt = up_proj(j + 1)
        cols = slice(j * FFN_CHUNK, (j + 1) * FFN_CHUNK)
        ge3 = ge.reshape(n + 4, V7X_SUBLANES, FFN_CHUNK)
        taps = [wdw_ref[k, :, cols] for k in range(FFN_CONV_WIDTH)]
        gc = _dwconv_rows(ge3, taps, FFN_CONV_WIDTH // 2, n, sub) + bdw_ref[:, cols]
        act = (_gelu_tanh(gc) * val.reshape(n, V7X_SUBLANES, FFN_CHUNK)).reshape(tm, FFN_CHUNK).astype(BF16)
        part = jnp.dot(act, wd_ref[cols, :], preferred_element_type=F32)
        acc = part if acc is None else acc + part
    o_ref[...] = _post_residual(x, acc, postg_ref[...], gate_ref[...])


def _proj_body(x_ref, gain_ref, shift_ref, scale_ref, w_ref, *o_refs, col0):
    h = _modnorm(x_ref[...], gain_ref[...], shift_ref[...], scale_ref[...]).astype(BF16)
    for t, o_ref in enumerate(o_refs):
        cols = slice(col0 + t * D_MODEL, col0 + (t + 1) * D_MODEL)
        o_ref[...] = jnp.dot(h, w_ref[:, cols], preferred_element_type=F32).astype(o_ref.dtype)


def _out_proj_body(a_ref, x_ref, w_ref, postg_ref, gate_ref, o_ref):
    y = jnp.dot(a_ref[...], w_ref[...], preferred_element_type=F32)
    o_ref[...] = _post_residual(x_ref[...], y, postg_ref[...], gate_ref[...])


def _bias_body(src_ref, o_ref):
    qc = lax.broadcasted_iota(jnp.int32, (GRID_W, ATTN_PAIR), 0)
    kc = lax.broadcasted_iota(jnp.int32, (GRID_W, ATTN_PAIR), 1) & (GRID_W - 1)
    c0 = jnp.clip(qc - NA_KW // 2, 0, GRID_W - NA_KW)
    in_win = (kc >= c0) & (kc < c0 + NA_KW)
    base_shift = V7X_LANES - (NA_KW - 1)
    for s in range(2 * NA_KH - 2):
        for e in range(2):
            src = jnp.broadcast_to(src_ref[s, e], (GRID_W, ATTN_PAIR))
            tile = pltpu.roll(src, base_shift, 1, stride=1, stride_axis=0)
            o_ref[s, e * GRID_W:(e + 1) * GRID_W, :] = jnp.where(in_win, tile, NEG_INF)


def _attn_body(q_ref, k_ref, v_ref, kc_ref, vc_ref, bias_ref, o_ref, *, rows, rows_per_step):
    n_loc = NA_KH * GRID_W
    lo = lax.broadcasted_iota(jnp.int32, (GRID_W, ATTN_PAIR), 1) < NA_HEAD_DIM
    nt = (((1,), (1,)), ((), ()))

    def scores(p, rr):
        lanes = slice(p * ATTN_PAIR, (p + 1) * ATTN_PAIR)
        r = pl.program_id(1) * rows_per_step + rr
        r0 = jnp.clip(r - NA_KH // 2, 0, rows - NA_KH)
        row_off = r0 - r + (NA_KH - 1)
        start = pl.multiple_of(r0 * GRID_W, GRID_W)
        qp = q_ref[rr * GRID_W:(rr + 1) * GRID_W, lanes]
        zero = jnp.zeros_like(qp)
        qs = jnp.concatenate([jnp.where(lo, qp, zero), jnp.where(lo, zero, qp)], axis=0)
        bias = jnp.concatenate([bias_ref[p, row_off + 2 * jj] for jj in range(NA_KH // 2)], axis=1)
        s_loc = lax.dot_general(qs, k_ref[pl.ds(start, n_loc), lanes], nt, preferred_element_type=F32) + bias
        s_ctx = lax.dot_general(qs, kc_ref[:, lanes], nt, preferred_element_type=F32)
        return s_loc, s_ctx, start

    units = [(p, rr) for p in range(NA_HEADS // 2) for rr in range(rows_per_step)]
    pending = [scores(*units[i]) for i in range(ATTN_LOOKAHEAD)]
    for idx, (p, rr) in enumerate(units):
        s_loc, s_ctx, start = pending.pop(0)
        if idx + ATTN_LOOKAHEAD < len(units):
            pending.append(scores(*units[idx + ATTN_LOOKAHEAD]))
        lanes = slice(p * ATTN_PAIR, (p + 1) * ATTN_PAIR)
        m = jnp.maximum(jnp.max(s_loc, axis=-1, keepdims=True), jnp.max(s_ctx, axis=-1, keepdims=True))
        p_loc = jnp.exp(s_loc - m).astype(BF16)
        p_ctx = jnp.exp(s_ctx - m).astype(BF16)
        vp = v_ref[pl.ds(start, n_loc), lanes]
        vcp = vc_ref[:, lanes]
        v_ext = jnp.concatenate([vp, jnp.ones_like(vp)], axis=1)
        vc_ext = jnp.concatenate([vcp, jnp.ones_like(vcp)], axis=1)
        o_ext = (jnp.dot(p_loc, v_ext, preferred_element_type=F32)
                 + jnp.dot(p_ctx, vc_ext, preferred_element_type=F32))
        o = o_ext[:, :ATTN_PAIR] / o_ext[:, ATTN_PAIR:ATTN_PAIR + 1]
        o_ref[rr * GRID_W:(rr + 1) * GRID_W, lanes] = jnp.where(lo, o[:GRID_W], o[GRID_W:]).astype(o_ref.dtype)


def _tile_rows(t, pref=512):
    return pref if t % pref == 0 else t


def _row_spec(tm, width=D_MODEL):
    return pl.BlockSpec((None, tm, width), lambda b, i: (b, i, 0))


def _halo_specs(tm, t):
    per = tm // HALO_ROWS
    last = t // HALO_ROWS - 1
    prev = pl.BlockSpec((None, HALO_ROWS, D_MODEL), lambda b, i: (b, jnp.maximum(i * per - 1, 0), 0))
    nxt = pl.BlockSpec((None, HALO_ROWS, D_MODEL), lambda b, i: (b, jnp.minimum((i + 1) * per, last), 0))
    return prev, nxt


def _const_spec(shape):
    nd = len(shape)
    return pl.BlockSpec(shape, lambda b, i: (0,) * nd, pipeline_mode=pl.Buffered(1))


def _vec_spec():
    return pl.BlockSpec((1, D_MODEL), lambda b, i: (0, 0))


def _mod_spec(layer, comp, ctx):
    if ctx is None:
        return pl.BlockSpec((None, None, None, 1, D_MODEL), lambda b, i: (layer, comp, b, 0, 0))
    return pl.BlockSpec((None, None, None, 1, D_MODEL), lambda b, i: (layer, comp, ctx, 0, 0))


def _params(nbytes):
    return pltpu.CompilerParams(dimension_semantics=("arbitrary", "arbitrary"), vmem_limit_bytes=_vmem_limit(nbytes))


def _ada_mod(cs, mod_w, mod_b):
    depth = mod_w.shape[0]
    blk = _nbytes((D_MODEL, D_MODEL), F32)
    return pl.pallas_call(
        _ada_body,
        grid=(depth, N_MOD),
        in_specs=[
            pl.BlockSpec((MOD_ROWS, D_MODEL), lambda l, k: (0, 0)),
            pl.BlockSpec((None, D_MODEL, D_MODEL), lambda l, k: (l, 0, k)),
            pl.BlockSpec((None, 1, D_MODEL), lambda l, k: (l, 0, k)),
        ],
        out_specs=pl.BlockSpec((None, None, MOD_ROWS, 1, D_MODEL), lambda l, k: (l, k, 0, 0, 0)),
        out_shape=jax.ShapeDtypeStruct((depth, N_MOD, MOD_ROWS, 1, D_MODEL), F32),
        compiler_params=pltpu.CompilerParams(dimension_semantics=("arbitrary", "arbitrary"),
                                             vmem_limit_bytes=_vmem_limit(4 * blk)),
        name="ada_mod",
    )(cs, mod_w, mod_b.reshape(depth, 1, N_MOD * D_MODEL))


def _pw1_glu(x, gain, mods, layer, ctx, w_bf, b):
    bsz, t, _ = x.shape
    tm = _tile_rows(t)
    est = 4 * _nbytes((tm, D_MODEL), F32) + _nbytes(w_bf.shape, BF16) + 3 * _nbytes((tm, 2 * D_MODEL), F32)
    return pl.pallas_call(
        _pw1_glu_body,
        grid=(bsz, t // tm),
        in_specs=[_row_spec(tm), _vec_spec(), _mod_spec(layer, 0, ctx), _mod_spec(layer, 1, ctx),
                  _const_spec(w_bf.shape), _const_spec(b.shape)],
        out_specs=_row_spec(tm),
        out_shape=jax.ShapeDtypeStruct((bsz, t, D_MODEL), F32),
        compiler_params=_params(est),
        name="pw1_glu",
    )(x, gain, mods, mods, w_bf, b)


def _conv_mix(u, x, wdw_b, bdw_b, ln_g, ln_b, w2_bf, b2, post_g, mods, layer, ctx):
    bsz, t, _ = x.shape
    tm = _tile_rows(t)
    n_tiles = t // tm
    prev, nxt = _halo_specs(tm, t)
    est = (8 * _nbytes((tm, D_MODEL), F32) + _nbytes(w2_bf.shape, BF16) + _nbytes(wdw_b.shape, F32)
           + 4 * _nbytes((tm, D_MODEL), F32))
    return pl.pallas_call(
        functools.partial(_conv_mix_body, tm=tm, n_tiles=n_tiles),
        grid=(bsz, n_tiles),
        in_specs=[prev, _row_spec(tm), nxt, _row_spec(tm),
                  _const_spec(wdw_b.shape), _const_spec(bdw_b.shape), _vec_spec(), _vec_spec(),
                  _const_spec(w2_bf.shape), _vec_spec(), _vec_spec(), _mod_spec(layer, 2, ctx)],
        out_specs=_row_spec(tm),
        out_shape=jax.ShapeDtypeStruct((bsz, t, D_MODEL), F32),
        scratch_shapes=[pltpu.VMEM((tm + 2 * HALO_ROWS, D_MODEL), F32), pltpu.VMEM((tm, D_MODEL), F32)],
        compiler_params=_params(est),
        name="conv_mix",
    )(u, u, u, x, wdw_b, bdw_b, ln_g, ln_b, w2_bf, b2, post_g, mods)


def _ffn(x, gain, post_g, mods, layer, ctx, wup, wdw_b, bdw_b, wd):
    bsz, t, _ = x.shape
    tm = _tile_rows(t, FFN_ROWS)
    n_tiles = t // tm
    prev, nxt = _halo_specs(tm, t)
    est = (8 * _nbytes((tm, D_MODEL), F32) + _nbytes(wup.shape, BF16) + _nbytes(wd.shape, BF16)
           + 8 * _nbytes((tm + 2 * HALO_ROWS, FFN_CHUNK), F32))
    return pl.pallas_call(
        functools.partial(_ffn_body, tm=tm, n_tiles=n_tiles),
        grid=(bsz, n_tiles),
        in_specs=[prev, _row_spec(tm), nxt, _vec_spec(), _mod_spec(layer, 3, ctx), _mod_spec(layer, 4, ctx),
                  _const_spec(wup.shape), _const_spec(wdw_b.shape), _const_spec(bdw_b.shape),
                  _const_spec(wd.shape), _vec_spec(), _mod_spec(layer, 5, ctx)],
        out_specs=_row_spec(tm),
        out_shape=jax.ShapeDtypeStruct((bsz, t, D_MODEL), F32),
        compiler_params=_params(est),
        name="conv_ffn",
    )(x, x, x, gain, mods, mods, wup, wdw_b, bdw_b, wd, post_g, mods)


def _proj(x, gain, mods, layer, ctx, w_bf, col0, n_out):
    bsz, t, _ = x.shape
    tm = _tile_rows(t)
    est = 2 * _nbytes((tm, D_MODEL), F32) + _nbytes(w_bf.shape, BF16) + (2 * n_out + 4) * _nbytes((tm, D_MODEL), F32)
    return pl.pallas_call(
        functools.partial(_proj_body, col0=col0),
        grid=(bsz, t // tm),
        in_specs=[_row_spec(tm), _vec_spec(), _mod_spec(layer, 0, ctx), _mod_spec(layer, 1, ctx),
                  _const_spec(w_bf.shape)],
        out_specs=[_row_spec(tm)] * n_out,
        out_shape=[jax.ShapeDtypeStruct((bsz, t, D_MODEL), BF16)] * n_out,
        compiler_params=_params(est),
        name="norm_proj",
    )(x, gain, mods, mods, w_bf)


def _out_proj(a, x, w_bf, post_g, mods, layer):
    bsz, t, _ = x.shape
    tm = _tile_rows(t)
    est = 6 * _nbytes((tm, D_MODEL), F32) + _nbytes(w_bf.shape, BF16)
    return pl.pallas_call(
        _out_proj_body,
        grid=(bsz, t // tm),
        in_specs=[_row_spec(tm), _row_spec(tm), _const_spec(w_bf.shape), _vec_spec(), _mod_spec(layer, 2, None)],
        out_specs=_row_spec(tm),
        out_shape=jax.ShapeDtypeStruct((bsz, t, D_MODEL), F32),
        compiler_params=_params(est),
        name="out_proj",
    )(a, x, w_bf, post_g, mods)


def _attention(q, k, v, kc, vc, bias):
    bsz, t, _ = q.shape
    rows = t // GRID_W
    c_len = kc.shape[1]
    rps = ATTN_ROWS_PER_STEP
    assert rows % rps == 0
    tq = rps * GRID_W
    est = (4 * _nbytes((t, D_MODEL), BF16) + 4 * _nbytes((c_len, D_MODEL), BF16)
           + _nbytes(bias.shape, F32) + 4 * _nbytes((tq, D_MODEL), BF16))
    return pl.pallas_call(
        functools.partial(_attn_body, rows=rows, rows_per_step=rps),
        grid=(bsz, rows // rps),
        in_specs=[
            pl.BlockSpec((None, tq, D_MODEL), lambda b, r: (b, r, 0)),
            pl.BlockSpec((None, t, D_MODEL), lambda b, r: (b, 0, 0)),
            pl.BlockSpec((None, t, D_MODEL), lambda b, r: (b, 0, 0)),
            pl.BlockSpec((None, c_len, D_MODEL), lambda b, r: (b, 0, 0)),
            pl.BlockSpec((None, c_len, D_MODEL), lambda b, r: (b, 0, 0)),
            _const_spec(bias.shape),
        ],
        out_specs=pl.BlockSpec((None, tq, D_MODEL), lambda b, r: (b, r, 0)),
        out_shape=jax.ShapeDtypeStruct((bsz, t, D_MODEL), BF16),
        compiler_params=_params(est),
        name="na_attention",
    )(q, k, v, kc, vc, bias)


def _attn_bias_table(rpb):
    n_pairs = NA_HEADS // 2
    n_s = 2 * NA_KH - 2
    rp = jnp.pad(rpb.astype(F32), ((0, 0), (0, 0), (0, GRID_W - (2 * NA_KW - 1))))
    src = jnp.concatenate([rp[:, :n_s], rp[:, 1:]], axis=-1)
    src = jnp.transpose(src.reshape(n_pairs, 2, n_s, ATTN_PAIR), (0, 2, 1, 3))[:, :, :, None, :]
    return pl.pallas_call(
        _bias_body,
        grid=(n_pairs,),
        in_specs=[pl.BlockSpec((None, n_s, 2, 1, ATTN_PAIR), lambda p: (p, 0, 0, 0, 0))],
        out_specs=pl.BlockSpec((None, n_s, 2 * GRID_W, ATTN_PAIR), lambda p: (p, 0, 0, 0)),
        out_shape=jax.ShapeDtypeStruct((n_pairs, n_s, 2 * GRID_W, ATTN_PAIR), F32),
        name="na_bias",
    )(src)


def _ffn_weights(w_up, w_dw, b_dw, w_down):
    wdw_b = jnp.broadcast_to(w_dw[:, None, :], (FFN_CONV_WIDTH, V7X_SUBLANES, D_FF))
    bdw_b = jnp.broadcast_to(b_dw[None, :], (V7X_SUBLANES, D_FF))
    return w_up.astype(BF16), wdw_b, bdw_b, w_down.astype(BF16)


def kernel(x, c, ctx, c_ctx, mod_w, mod_b, mix_pre_g, mix_post_g, ffn_pre_g, ffn_post_g, cv_w_pw1, cv_b_pw1, cv_w_dw, cv_b_dw, cv_ln_g, cv_ln_b, cv_w_pw2, cv_b_pw2, na_w_qkv, na_w_o, na_rpb, ffn_w_up, ffn_w_dw, ffn_b_dw, ffn_w_down):
    bsz = x.shape[0]
    assert bsz + 1 <= MOD_ROWS and x.shape[2] == D_MODEL and mod_w.shape[0] == DEPTH
    ctx_row = bsz
    cs = jnp.concatenate([c, c_ctx[None, :], jnp.zeros((MOD_ROWS - bsz - 1, D_MODEL), F32)], axis=0)
    mods = _ada_mod(cs, mod_w, mod_b)

    h_ctx = ctx
    for i in range(DEPTH):
        last = i == DEPTH - 1
        j = i // 2
        use_na = (i % 2) == 1
        pre_g = mix_pre_g[i][None, :]
        post_g = mix_post_g[i][None, :]
        ffn = _ffn_weights(ffn_w_up[i], ffn_w_dw[i], ffn_b_dw[i], ffn_w_down[i])
        if use_na:
            scale = NA_HEAD_DIM ** -0.5
            col_scale = jnp.concatenate([jnp.full((D_MODEL,), scale, F32), jnp.ones((2 * D_MODEL,), F32)])
            w_qkv = (na_w_qkv[j] * col_scale[None, :]).astype(BF16)
            q, k, v = _proj(x, pre_g, mods, i, None, w_qkv, 0, 3)
            kc, vc = _proj(h_ctx, pre_g, mods, i, ctx_row, w_qkv, D_MODEL, 2)
            attn = _attention(q, k, v, kc, vc, _attn_bias_table(na_rpb[j]))
            x = _out_proj(attn, x, na_w_o[j].astype(BF16), post_g, mods, i)
            assert last, "context queries are only needed when another layer follows"
        else:
            w1 = cv_w_pw1[j].astype(BF16)
            b1 = cv_b_pw1[j][None, :]
            wdw_b = jnp.broadcast_to(cv_w_dw[j][:, None, :], (CONV_WIDTH, V7X_SUBLANES, D_MODEL))
            bdw_b = jnp.broadcast_to(cv_b_dw[j][None, :], (V7X_SUBLANES, D_MODEL))
            w2 = cv_w_pw2[j].astype(BF16)
            cv = (wdw_b, bdw_b, cv_ln_g[j][None, :], cv_ln_b[j][None, :], w2, cv_b_pw2[j][None, :], post_g)
            u = _pw1_glu(x, pre_g, mods, i, None, w1, b1)
            x = _conv_mix(u, x, *cv, mods, i, None)
            if not last:
                u_ctx = _pw1_glu(h_ctx, pre_g, mods, i, ctx_row, w1, b1)
                h_ctx = _conv_mix(u_ctx, h_ctx, *cv, mods, i, ctx_row)
        x = _ffn(x, ffn_pre_g[i][None, :], ffn_post_g[i][None, :], mods, i, None, *ffn)
        if not last:
            h_ctx = _ffn(h_ctx, ffn_pre_g[i][None, :], ffn_post_g[i][None, :], mods, i, ctx_row, *ffn)
    return x
```

```python
import functools
import math

import numpy as np
import jax
import jax.numpy as jnp
from jax import lax
from jax.experimental import pallas as pl
from jax.experimental.pallas import tpu as pltpu

D_MODEL = 1024
DEPTH = 2
GRID_W = 64
CONV_WIDTH = 31
NA_HEADS = 16
NA_HEAD_DIM = D_MODEL // NA_HEADS
NA_KH = 8
NA_KW = 16
D_FF = 2816
FFN_CONV_WIDTH = 3
N_MOD = 6
RMS_EPS = 1e-6
LN_EPS = 1e-5
NEG_INF = -1e30

V7X_LANES = 128
V7X_SUBLANES = 8
V7X_BF16_ROWS = 16
V7X_VMEM_BYTES = 64 * 1024 * 1024

HALO_ROWS = 2 * V7X_SUBLANES
MOD_ROWS = 16
FFN_CHUNK = 256
FFN_ROWS = 512
FFN_TAIL_BLOCKS = 2
CONV_ROWS = 64
ATTN_PAIR = 2 * NA_HEAD_DIM
ATTN_ROWS_PER_STEP = 4
ATTN_LOOKAHEAD = 3

F32 = jnp.float32
BF16 = jnp.bfloat16


def _vmem_limit(nbytes):
    return int(min(V7X_VMEM_BYTES - 6 * 1024 * 1024, max(2 * nbytes, 24 * 1024 * 1024)))


def _nbytes(shape, dtype):
    return math.prod(shape) * jnp.dtype(dtype).itemsize


def _modnorm(xf, gain, shift, scale):
    ms = jnp.mean(xf * xf, axis=-1, keepdims=True)
    return (xf * lax.rsqrt(ms + RMS_EPS)) * (gain * (1.0 + scale)) + shift


def _post_residual(xf, y, post_g, gate):
    ms = jnp.mean(y * y, axis=-1, keepdims=True)
    return xf + gate * ((y * lax.rsqrt(ms + RMS_EPS)) * post_g)


def _gelu_tanh(x):
    c = math.sqrt(2.0 / math.pi)
    return x * (0.5 * (1.0 + jnp.tanh(c * (x + 0.044715 * (x * x * x)))))


def _dwconv_rows(xe3, taps, pad, n, sub):
    out = None
    for r in range(V7X_SUBLANES):
        offs = [d for d in range(-pad, pad + 1) if d % V7X_SUBLANES == r]
        if not offs:
            continue
        rr = xe3 if r == 0 else pltpu.roll(xe3, V7X_SUBLANES - r, axis=1)
        m = n if r == 0 else n + 1
        part = None
        for d in offs:
            a = d // V7X_SUBLANES
            term = taps[d + pad] * rr[2 + a: 2 + a + m]
            part = term if part is None else part + term
        if r != 0:
            part = jnp.where(sub < V7X_SUBLANES - r, part[0:n], part[1:n + 1])
        out = part if out is None else out + part
    return out


def _ada_body(cs_ref, w_ref, b_ref, o_ref):
    s = cs_ref[...]
    s = s * jax.nn.sigmoid(s)
    m = jnp.dot(s.astype(BF16), w_ref[...].astype(BF16), preferred_element_type=F32) + b_ref[...]
    for row in range(MOD_ROWS):
        o_ref[row] = m[row:row + 1, :]


def _pw1_glu_body(x_ref, gain_ref, shift_ref, scale_ref, w_ref, b_ref, o_ref, *, tm):
    gain, shift, scale = gain_ref[...], shift_ref[...], scale_ref[...]
    rb = tm // 2
    hs = [_modnorm(x_ref[q * rb:(q + 1) * rb], gain, shift, scale).astype(BF16) for q in range(2)]
    for q, h in enumerate(hs):
        a = jnp.dot(h, w_ref[:, :D_MODEL], preferred_element_type=F32) + b_ref[:, :D_MODEL]
        g = jnp.dot(h, w_ref[:, D_MODEL:], preferred_element_type=F32) + b_ref[:, D_MODEL:]
        o_ref[q * rb:(q + 1) * rb] = a * jax.nn.sigmoid(g)


def _conv_mix_body(up_ref, u_ref, un_ref, x_ref, wdw_ref, bdw_ref, lng_ref, lnb_ref, w2_ref, b2_ref,
                   postg_ref, gate_ref, o_ref, ue_scr, cv_scr, *, tm, n_tiles):
    i = pl.program_id(1)
    ue_scr[0:HALO_ROWS] = jnp.where(i > 0, up_ref[...], 0.0)
    ue_scr[HALO_ROWS:HALO_ROWS + tm] = u_ref[...]
    ue_scr[HALO_ROWS + tm:] = jnp.where(i < n_tiles - 1, un_ref[...], 0.0)

    nv = CONV_ROWS // V7X_SUBLANES
    sub = lax.broadcasted_iota(jnp.int32, (nv, V7X_SUBLANES, V7X_LANES), 1)
    pad = CONV_WIDTH // 2

    def conv_step(ci, carry):
        base = pl.multiple_of(ci * CONV_ROWS, CONV_ROWS)
        for lc in range(0, D_MODEL, V7X_LANES):
            lanes = slice(lc, lc + V7X_LANES)
            xe3 = ue_scr[pl.ds(base, CONV_ROWS + 2 * HALO_ROWS), lanes].reshape(nv + 4, V7X_SUBLANES, V7X_LANES)
            taps = [wdw_ref[k, :, lanes] for k in range(CONV_WIDTH)]
            y = _dwconv_rows(xe3, taps, pad, nv, sub) + bdw_ref[:, lanes]
            cv_scr[pl.ds(base, CONV_ROWS), lanes] = y.reshape(CONV_ROWS, V7X_LANES)
        return carry

    lax.fori_loop(0, tm // CONV_ROWS, conv_step, 0)

    u = cv_scr[...]
    uc = u - jnp.mean(u, axis=-1, keepdims=True)
    ln = (uc * lax.rsqrt(jnp.mean(uc * uc, axis=-1, keepdims=True) + LN_EPS)) * lng_ref[...] + lnb_ref[...]
    act = (ln * jax.nn.sigmoid(ln)).astype(BF16)
    y = jnp.dot(act, w2_ref[...], preferred_element_type=F32) + b2_ref[...]
    o_ref[...] = _post_residual(x_ref[...], y, postg_ref[...], gate_ref[...])


def _ffn_body(xp_ref, x_ref, xn_ref, gain_ref, shift_ref, scale_ref, wup_ref, wdw_ref, bdw_ref, wd_ref,
              postg_ref, gate_ref, o_ref, *, tm, n_tiles):
    i = pl.program_id(1)
    gain, shift, scale = gain_ref[...], shift_ref[...], scale_ref[...]
    x = x_ref[...]
    hp = jnp.where(i > 0, _modnorm(xp_ref[...], gain, shift, scale), 0.0).astype(BF16)
    hn = jnp.where(i < n_tiles - 1, _modnorm(xn_ref[...], gain, shift, scale), 0.0).astype(BF16)
    half = tm // 2
    h_lo = _modnorm(x[:half], gain, shift, scale).astype(BF16)
    h_hi = _modnorm(x[half:], gain, shift, scale).astype(BF16)
    h = jnp.concatenate([h_lo, h_hi], axis=0)
    he = jnp.concatenate([hp, h, hn], axis=0)

    n = tm // V7X_SUBLANES
    sub = lax.broadcasted_iota(jnp.int32, (n, V7X_SUBLANES, FFN_CHUNK), 1)
    n_chunks = D_FF // FFN_CHUNK

    def up_proj(j):
        cols = slice(j * FFN_CHUNK, (j + 1) * FFN_CHUNK)
        vcols = slice(D_FF + j * FFN_CHUNK, D_FF + (j + 1) * FFN_CHUNK)
        if j == 0:
            lo_rows = HALO_ROWS + half
            ge_lo = jnp.dot(he[:lo_rows], wup_ref[:, cols], preferred_element_type=F32)
            val_lo = jnp.dot(h_lo, wup_ref[:, vcols], preferred_element_type=F32)
            ge_hi = jnp.dot(he[lo_rows:], wup_ref[:, cols], preferred_element_type=F32)
            val_hi = jnp.dot(h_hi, wup_ref[:, vcols], preferred_element_type=F32)
            return jnp.concatenate([ge_lo, ge_hi], axis=0), jnp.concatenate([val_lo, val_hi], axis=0)
        ge = jnp.dot(he, wup_ref[:, cols], preferred_element_type=F32)
        val = jnp.dot(h, wup_ref[:, vcols], preferred_element_type=F32)
        return ge, val

    acts = []
    nxt = up_proj(0)
    for j in range(n_chunks):
        ge, val = nxt
        if j + 1 < n_chunks:
            nxt = up_proj(j + 1)
        cols = slice(j * FFN_CHUNK, (j + 1) * FFN_CHUNK)
        ge3 = ge.reshape(n + 4, V7X_SUBLANES, FFN_CHUNK)
        taps = [wdw_ref[k, :, cols] for k in range(FFN_CONV_WIDTH)]
        gc = _dwconv_rows(ge3, taps, FFN_CONV_WIDTH // 2, n, sub) + bdw_ref[:, cols]
        acts.append((_gelu_tanh(gc) * val.reshape(n, V7X_SUBLANES, FFN_CHUNK)).reshape(tm, FFN_CHUNK).astype(BF16))
    act = jnp.concatenate(acts, axis=1)
    rb = tm // FFN_TAIL_BLOCKS
    for q in range(FFN_TAIL_BLOCKS):
        rows = slice(q * rb, (q + 1) * rb)
        y = jnp.dot(act[rows], wd_ref[...], preferred_element_type=F32)
        o_ref[rows] = _post_residual(x[rows], y, postg_ref[...], gate_ref[...])


def _proj_body(x_ref, gain_ref, shift_ref, scale_ref, w_ref, *o_refs, col0):
    h = _modnorm(x_ref[...], gain_ref[...], shift_ref[...], scale_ref[...]).astype(BF16)
    for t, o_ref in enumerate(o_refs):
        cols = slice(col0 + t * D_MODEL, col0 + (t + 1) * D_MODEL)
        o_ref[...] = jnp.dot(h, w_ref[:, cols], preferred_element_type=F32).astype(o_ref.dtype)


def _out_proj_body(a_ref, x_ref, w_ref, postg_ref, gate_ref, o_ref):
    y = jnp.dot(a_ref[...], w_ref[...], preferred_element_type=F32)
    o_ref[...] = _post_residual(x_ref[...], y, postg_ref[...], gate_ref[...])


def _bias_body(src_ref, o_ref):
    qc = lax.broadcasted_iota(jnp.int32, (GRID_W, ATTN_PAIR), 0)
    kc = lax.broadcasted_iota(jnp.int32, (GRID_W, ATTN_PAIR), 1) & (GRID_W - 1)
    c0 = jnp.clip(qc - NA_KW // 2, 0, GRID_W - NA_KW)
    in_win = (kc >= c0) & (kc < c0 + NA_KW)
    base_shift = V7X_LANES - (NA_KW - 1)
    for s in range(2 * NA_KH - 2):
        for e in range(2):
            src = jnp.broadcast_to(src_ref[s, e], (GRID_W, ATTN_PAIR))
            tile = pltpu.roll(src, base_shift, 1, stride=1, stride_axis=0)
            o_ref[s, e * GRID_W:(e + 1) * GRID_W, :] = jnp.where(in_win, tile, NEG_INF)


def _attn_body(q_ref, k_ref, v_ref, kc_ref, vc_ref, bias_ref, o_ref, *, rows, rows_per_step):
    n_loc = NA_KH * GRID_W
    lo = lax.broadcasted_iota(jnp.int32, (GRID_W, ATTN_PAIR), 1) < NA_HEAD_DIM
    nt = (((1,), (1,)), ((), ()))

    def scores(p, rr):
        lanes = slice(p * ATTN_PAIR, (p + 1) * ATTN_PAIR)
        r = pl.program_id(1) * rows_per_step + rr
        r0 = jnp.clip(r - NA_KH // 2, 0, rows - NA_KH)
        row_off = r0 - r + (NA_KH - 1)
        start = pl.multiple_of(r0 * GRID_W, GRID_W)
        qp = q_ref[rr * GRID_W:(rr + 1) * GRID_W, lanes]
        zero = jnp.zeros_like(qp)
        qs = jnp.concatenate([jnp.where(lo, qp, zero), jnp.where(lo, zero, qp)], axis=0)
        bias = jnp.concatenate([bias_ref[p, row_off + 2 * jj] for jj in range(NA_KH // 2)], axis=1)
        k_all = jnp.concatenate([k_ref[pl.ds(start, n_loc), lanes], kc_ref[:, lanes]], axis=0)
        s = lax.dot_general(qs, k_all, nt, preferred_element_type=F32)
        return s[:, :n_loc] + bias, s[:, n_loc:], start

    units = [(p, rr) for p in range(NA_HEADS // 2) for rr in range(rows_per_step)]
    pending = [scores(*units[i]) for i in range(ATTN_LOOKAHEAD)]
    for idx, (p, rr) in enumerate(units):
        s_loc, s_ctx, start = pending.pop(0)
        if idx + ATTN_LOOKAHEAD < len(units):
            pending.append(scores(*units[idx + ATTN_LOOKAHEAD]))
        lanes = slice(p * ATTN_PAIR, (p + 1) * ATTN_PAIR)
        m = jnp.maximum(jnp.max(s_loc, axis=-1, keepdims=True), jnp.max(s_ctx, axis=-1, keepdims=True))
        probs = jnp.concatenate([jnp.exp(s_loc - m).astype(BF16), jnp.exp(s_ctx - m).astype(BF16)], axis=1)
        v_all = jnp.concatenate([v_ref[pl.ds(start, n_loc), lanes], vc_ref[:, lanes]], axis=0)
        v_ext = jnp.concatenate([v_all, jnp.ones_like(v_all)], axis=1)
        o_ext = jnp.dot(probs, v_ext, preferred_element_type=F32)
        o = o_ext[:, :ATTN_PAIR] / o_ext[:, ATTN_PAIR:ATTN_PAIR + 1]
        o_ref[rr * GRID_W:(rr + 1) * GRID_W, lanes] = jnp.where(lo, o[:GRID_W], o[GRID_W:]).astype(o_ref.dtype)


def _tile_rows(t, pref=512):
    return pref if t % pref == 0 else t


def _row_spec(tm, width=D_MODEL):
    return pl.BlockSpec((None, tm, width), lambda b, i: (b, i, 0))


def _halo_specs(tm, t):
    per = tm // HALO_ROWS
    last = t // HALO_ROWS - 1
    prev = pl.BlockSpec((None, HALO_ROWS, D_MODEL), lambda b, i: (b, jnp.maximum(i * per - 1, 0), 0))
    nxt = pl.BlockSpec((None, HALO_ROWS, D_MODEL), lambda b, i: (b, jnp.minimum((i + 1) * per, last), 0))
    return prev, nxt


def _const_spec(shape):
    nd = len(shape)
    return pl.BlockSpec(shape, lambda b, i: (0,) * nd, pipeline_mode=pl.Buffered(1))


def _vec_spec():
    return pl.BlockSpec((1, D_MODEL), lambda b, i: (0, 0))


def _mod_spec(layer, comp, ctx):
    if ctx is None:
        return pl.BlockSpec((None, None, None, 1, D_MODEL), lambda b, i: (layer, comp, b, 0, 0))
    return pl.BlockSpec((None, None, None, 1, D_MODEL), lambda b, i: (layer, comp, ctx, 0, 0))


def _params(nbytes):
    return pltpu.CompilerParams(dimension_semantics=("arbitrary", "arbitrary"), vmem_limit_bytes=_vmem_limit(nbytes))


def _ada_mod(cs, mod_w, mod_b):
    depth = mod_w.shape[0]
    blk = _nbytes((D_MODEL, D_MODEL), F32)
    return pl.pallas_call(
        _ada_body,
        grid=(depth, N_MOD),
        in_specs=[
            pl.BlockSpec((MOD_ROWS, D_MODEL), lambda l, k: (0, 0)),
            pl.BlockSpec((None, D_MODEL, D_MODEL), lambda l, k: (l, 0, k)),
            pl.BlockSpec((None, 1, D_MODEL), lambda l, k: (l, 0, k)),
        ],
        out_specs=pl.BlockSpec((None, None, MOD_ROWS, 1, D_MODEL), lambda l, k: (l, k, 0, 0, 0)),
        out_shape=jax.ShapeDtypeStruct((depth, N_MOD, MOD_ROWS, 1, D_MODEL), F32),
        compiler_params=pltpu.CompilerParams(dimension_semantics=("arbitrary", "arbitrary"),
                                             vmem_limit_bytes=_vmem_limit(4 * blk)),
        name="ada_mod",
    )(cs, mod_w, mod_b.reshape(depth, 1, N_MOD * D_MODEL))


def _pw1_glu(x, gain, mods, layer, ctx, w_bf, b):
    bsz, t, _ = x.shape
    tm = _tile_rows(t)
    est = 4 * _nbytes((tm, D_MODEL), F32) + _nbytes(w_bf.shape, BF16) + 3 * _nbytes((tm, 2 * D_MODEL), F32)
    return pl.pallas_call(
        functools.partial(_pw1_glu_body, tm=tm),
        grid=(bsz, t // tm),
        in_specs=[_row_spec(tm), _vec_spec(), _mod_spec(layer, 0, ctx), _mod_spec(layer, 1, ctx),
                  _const_spec(w_bf.shape), _const_spec(b.shape)],
        out_specs=_row_spec(tm),
        out_shape=jax.ShapeDtypeStruct((bsz, t, D_MODEL), F32),
        compiler_params=_params(est),
        name="pw1_glu",
    )(x, gain, mods, mods, w_bf, b)


def _conv_mix(u, x, wdw_b, bdw_b, ln_g, ln_b, w2_bf, b2, post_g, mods, layer, ctx):
    bsz, t, _ = x.shape
    tm = _tile_rows(t)
    n_tiles = t // tm
    prev, nxt = _halo_specs(tm, t)
    est = (8 * _nbytes((tm, D_MODEL), F32) + _nbytes(w2_bf.shape, BF16) + _nbytes(wdw_b.shape, F32)
           + 4 * _nbytes((tm, D_MODEL), F32))
    return pl.pallas_call(
        functools.partial(_conv_mix_body, tm=tm, n_tiles=n_tiles),
        grid=(bsz, n_tiles),
        in_specs=[prev, _row_spec(tm), nxt, _row_spec(tm),
                  _const_spec(wdw_b.shape), _const_spec(bdw_b.shape), _vec_spec(), _vec_spec(),
                  _const_spec(w2_bf.shape), _vec_spec(), _vec_spec(), _mod_spec(layer, 2, ctx)],
        out_specs=_row_spec(tm),
        out_shape=jax.ShapeDtypeStruct((bsz, t, D_MODEL), F32),
        scratch_shapes=[pltpu.VMEM((tm + 2 * HALO_ROWS, D_MODEL), F32), pltpu.VMEM((tm, D_MODEL), F32)],
        compiler_params=_params(est),
        name="conv_mix",
    )(u, u, u, x, wdw_b, bdw_b, ln_g, ln_b, w2_bf, b2, post_g, mods)


def _layer_spec(stacked_shape, layer):
    nd = len(stacked_shape) - 1
    return pl.BlockSpec((None,) + tuple(stacked_shape[1:]), lambda b, i: (layer,) + (0,) * nd,
                        pipeline_mode=pl.Buffered(1))


def _ffn(x, gain, post_g, mods, layer, ctx, wup, wdw_b, bdw_b, wd):
    bsz, t, _ = x.shape
    tm = _tile_rows(t, FFN_ROWS)
    n_tiles = t // tm
    prev, nxt = _halo_specs(tm, t)
    est = (8 * _nbytes((tm, D_MODEL), F32) + _nbytes(wup.shape[1:], BF16) + _nbytes(wd.shape[1:], BF16)
           + _nbytes((tm, D_FF), BF16) + 8 * _nbytes((tm + 2 * HALO_ROWS, FFN_CHUNK), F32))
    return pl.pallas_call(
        functools.partial(_ffn_body, tm=tm, n_tiles=n_tiles),
        grid=(bsz, n_tiles),
        in_specs=[prev, _row_spec(tm), nxt, _vec_spec(), _mod_spec(layer, 3, ctx), _mod_spec(layer, 4, ctx),
                  _layer_spec(wup.shape, layer), _const_spec(wdw_b.shape), _const_spec(bdw_b.shape),
                  _layer_spec(wd.shape, layer), _vec_spec(), _mod_spec(layer, 5, ctx)],
        out_specs=_row_spec(tm),
        out_shape=jax.ShapeDtypeStruct((bsz, t, D_MODEL), F32),
        compiler_params=_params(est),
        name="conv_ffn",
    )(x, x, x, gain, mods, mods, wup, wdw_b, bdw_b, wd, post_g, mods)


def _proj(x, gain, mods, layer, ctx, w_bf, col0, n_out):
    bsz, t, _ = x.shape
    tm = _tile_rows(t)
    est = 2 * _nbytes((tm, D_MODEL), F32) + _nbytes(w_bf.shape, BF16) + (2 * n_out + 4) * _nbytes((tm, D_MODEL), F32)
    return pl.pallas_call(
        functools.partial(_proj_body, col0=col0),
        grid=(bsz, t // tm),
        in_specs=[_row_spec(tm), _vec_spec(), _mod_spec(layer, 0, ctx), _mod_spec(layer, 1, ctx),
                  _const_spec(w_bf.shape)],
        out_specs=[_row_spec(tm)] * n_out,
        out_shape=[jax.ShapeDtypeStruct((bsz, t, D_MODEL), BF16)] * n_out,
        compiler_params=_params(est),
        name="norm_proj",
    )(x, gain, mods, mods, w_bf)


def _out_proj(a, x, w_bf, post_g, mods, layer):
    bsz, t, _ = x.shape
    tm = _tile_rows(t)
    est = 6 * _nbytes((tm, D_MODEL), F32) + _nbytes(w_bf.shape, BF16)
    return pl.pallas_call(
        _out_proj_body,
        grid=(bsz, t // tm),
        in_specs=[_row_spec(tm), _row_spec(tm), _const_spec(w_bf.shape), _vec_spec(), _mod_spec(layer, 2, None)],
        out_specs=_row_spec(tm),
        out_shape=jax.ShapeDtypeStruct((bsz, t, D_MODEL), F32),
        compiler_params=_params(est),
        name="out_proj",
    )(a, x, w_bf, post_g, mods)


def _attention(q, k, v, kc, vc, bias):
    bsz, t, _ = q.shape
    rows = t // GRID_W
    c_len = kc.shape[1]
    rps = ATTN_ROWS_PER_STEP
    assert rows % rps == 0
    tq = rps * GRID_W
    est = (4 * _nbytes((t, D_MODEL), BF16) + 4 * _nbytes((c_len, D_MODEL), BF16)
           + _nbytes(bias.shape, F32) + 4 * _nbytes((tq, D_MODEL), BF16))
    return pl.pallas_call(
        functools.partial(_attn_body, rows=rows, rows_per_step=rps),
        grid=(bsz, rows // rps),
        in_specs=[
            pl.BlockSpec((None, tq, D_MODEL), lambda b, r: (b, r, 0)),
            pl.BlockSpec((None, t, D_MODEL), lambda b, r: (b, 0, 0)),
            pl.BlockSpec((None, t, D_MODEL), lambda b, r: (b, 0, 0)),
            pl.BlockSpec((None, c_len, D_MODEL), lambda b, r: (b, 0, 0)),
            pl.BlockSpec((None, c_len, D_MODEL), lambda b, r: (b, 0, 0)),
            _const_spec(bias.shape),
        ],
        out_specs=pl.BlockSpec((None, tq, D_MODEL), lambda b, r: (b, r, 0)),
        out_shape=jax.ShapeDtypeStruct((bsz, t, D_MODEL), BF16),
        compiler_params=_params(est),
        name="na_attention",
    )(q, k, v, kc, vc, bias)


def _attn_bias_table(rpb):
    n_pairs = NA_HEADS // 2
    n_s = 2 * NA_KH - 2
    rp = jnp.pad(rpb.astype(F32), ((0, 0), (0, 0), (0, GRID_W - (2 * NA_KW - 1))))
    src = jnp.concatenate([rp[:, :n_s], rp[:, 1:]], axis=-1)
    src = jnp.transpose(src.reshape(n_pairs, 2, n_s, ATTN_PAIR), (0, 2, 1, 3))[:, :, :, None, :]
    return pl.pallas_call(
        _bias_body,
        grid=(n_pairs,),
        in_specs=[pl.BlockSpec((None, n_s, 2, 1, ATTN_PAIR), lambda p: (p, 0, 0, 0, 0))],
        out_specs=pl.BlockSpec((None, n_s, 2 * GRID_W, ATTN_PAIR), lambda p: (p, 0, 0, 0)),
        out_shape=jax.ShapeDtypeStruct((n_pairs, n_s, 2 * GRID_W, ATTN_PAIR), F32),
        name="na_bias",
    )(src)


def _ffn_conv_params(w_dw, b_dw):
    wdw_b = jnp.broadcast_to(w_dw[:, None, :], (FFN_CONV_WIDTH, V7X_SUBLANES, D_FF))
    bdw_b = jnp.broadcast_to(b_dw[None, :], (V7X_SUBLANES, D_FF))
    return wdw_b, bdw_b


def kernel(x, c, ctx, c_ctx, mod_w, mod_b, mix_pre_g, mix_post_g, ffn_pre_g, ffn_post_g, cv_w_pw1, cv_b_pw1, cv_w_dw, cv_b_dw, cv_ln_g, cv_ln_b, cv_w_pw2, cv_b_pw2, na_w_qkv, na_w_o, na_rpb, ffn_w_up, ffn_w_dw, ffn_b_dw, ffn_w_down):
    bsz = x.shape[0]
    assert bsz + 1 <= MOD_ROWS and x.shape[2] == D_MODEL and mod_w.shape[0] == DEPTH
    ctx_row = bsz
    cs = jnp.concatenate([c, c_ctx[None, :], jnp.zeros((MOD_ROWS - bsz - 1, D_MODEL), F32)], axis=0)
    mods = _ada_mod(cs, mod_w, mod_b)
    wup_all = ffn_w_up.astype(BF16)
    wd_all = ffn_w_down.astype(BF16)

    h_ctx = ctx
    for i in range(DEPTH):
        last = i == DEPTH - 1
        j = i // 2
        use_na = (i % 2) == 1
        pre_g = mix_pre_g[i][None, :]
        post_g = mix_post_g[i][None, :]
        wdw3, bdw3 = _ffn_conv_params(ffn_w_dw[i], ffn_b_dw[i])
        ffn = (wup_all, wdw3, bdw3, wd_all)
        if use_na:
            scale = NA_HEAD_DIM ** -0.5
            col_scale = jnp.concatenate([jnp.full((D_MODEL,), scale, F32), jnp.ones((2 * D_MODEL,), F32)])
            w_qkv = (na_w_qkv[j] * col_scale[None, :]).astype(BF16)
            q, k, v = _proj(x, pre_g, mods, i, None, w_qkv, 0, 3)
            kc, vc = _proj(h_ctx, pre_g, mods, i, ctx_row, w_qkv, D_MODEL, 2)
            attn = _attention(q, k, v, kc, vc, _attn_bias_table(na_rpb[j]))
            x = _out_proj(attn, x, na_w_o[j].astype(BF16), post_g, mods, i)
            assert last, "context queries are only needed when another layer follows"
        else:
            w1 = cv_w_pw1[j].astype(BF16)
            b1 = cv_b_pw1[j][None, :]
            wdw_b = jnp.broadcast_to(cv_w_dw[j][:, None, :], (CONV_WIDTH, V7X_SUBLANES, D_MODEL))
            bdw_b = jnp.broadcast_to(cv_b_dw[j][None, :], (V7X_SUBLANES, D_MODEL))
            w2 = cv_w_pw2[j].astype(BF16)
            cv = (wdw_b, bdw_b, cv_ln_g[j][None, :], cv_ln_b[j][None, :], w2, cv_b_pw2[j][None, :], post_g)
            u = _pw1_glu(x, pre_g, mods, i, None, w1, b1)
            x = _conv_mix(u, x, *cv, mods, i, None)
            if not last:
                u_ctx = _pw1_glu(h_ctx, pre_g, mods, i, ctx_row, w1, b1)
                h_ctx = _conv_mix(u_ctx, h_ctx, *cv, mods, i, ctx_row)
        x = _ffn(x, ffn_pre_g[i][None, :], ffn_post_g[i][None, :], mods, i, None, *ffn)
        if not last:
            h_ctx = _ffn(h_ctx, ffn_pre_g[i][None, :], ffn_post_g[i][None, :], mods, i, ctx_row, *ffn)
    return x
```

```python
import functools
import math

import jax
import jax.numpy as jnp
from jax import lax
from jax.experimental import pallas as pl
from jax.experimental.pallas import tpu as pltpu

D_MODEL = 1024
DEPTH = 2
GRID_W = 64
CONV_WIDTH = 31
NA_HEADS = 16
NA_HEAD_DIM = D_MODEL // NA_HEADS
NA_KH = 8
NA_KW = 16
D_FF = 2816
FFN_CONV_WIDTH = 3
N_MOD = 6
RMS_EPS = 1e-6
LN_EPS = 1e-5
NEG_INF = -1e30

V7X_LANES = 128
V7X_SUBLANES = 8
V7X_VMEM_BYTES = 64 * 1024 * 1024

HALO_ROWS = 2 * V7X_SUBLANES
MOD_ROWS = 16
FFN_CHUNK = 256
FFN_ROWS = 512
FFN_TAIL_BLOCKS = 2
CONV_ROWS = 256
PROJ_ROWS = 1024
ATTN_PAIR = 2 * NA_HEAD_DIM
ATTN_ROWS_PER_STEP = 4
ATTN_LOOKAHEAD = 3

F32 = jnp.float32
BF16 = jnp.bfloat16


def _vmem_limit(nbytes):
    return int(min(V7X_VMEM_BYTES - 6 * 1024 * 1024, max(2 * nbytes, 24 * 1024 * 1024)))


def _nbytes(shape, dtype):
    return math.prod(shape) * jnp.dtype(dtype).itemsize


def _modnorm(xf, gain, shift, scale):
    ms = jnp.mean(xf * xf, axis=-1, keepdims=True)
    return (xf * lax.rsqrt(ms + RMS_EPS)) * (gain * (1.0 + scale)) + shift


def _post_residual(xf, y, post_g, gate):
    ms = jnp.mean(y * y, axis=-1, keepdims=True)
    return xf + gate * ((y * lax.rsqrt(ms + RMS_EPS)) * post_g)


def _gelu_tanh(x):
    c = math.sqrt(2.0 / math.pi)
    return x * (0.5 * (1.0 + jnp.tanh(c * (x + 0.044715 * (x * x * x)))))


def _dwconv_rows(xe3, taps, pad, n, sub):
    out = None
    for r in range(V7X_SUBLANES):
        offs = [d for d in range(-pad, pad + 1) if d % V7X_SUBLANES == r]
        if not offs:
            continue
        rr = xe3 if r == 0 else pltpu.roll(xe3, V7X_SUBLANES - r, axis=1)
        m = n if r == 0 else n + 1
        part = None
        for d in offs:
            a = d // V7X_SUBLANES
            term = taps[d + pad] * rr[2 + a: 2 + a + m]
            part = term if part is None else part + term
        if r != 0:
            part = jnp.where(sub < V7X_SUBLANES - r, part[0:n], part[1:n + 1])
        out = part if out is None else out + part
    return out


def _ada_body(cs_ref, w_ref, b_ref, o_ref):
    s = cs_ref[...]
    s = s * jax.nn.sigmoid(s)
    m = jnp.dot(s.astype(BF16), w_ref[...].astype(BF16), preferred_element_type=F32) + b_ref[...]
    for row in range(MOD_ROWS):
        o_ref[row] = m[row:row + 1, :]


def _pw1_glu_body(x_ref, gain_ref, shift_ref, scale_ref, w_ref, b_ref, o_ref, *, tm):
    gain, shift, scale = gain_ref[...], shift_ref[...], scale_ref[...]
    rb = tm // 2
    hs = [_modnorm(x_ref[q * rb:(q + 1) * rb], gain, shift, scale).astype(BF16) for q in range(2)]
    for q, h in enumerate(hs):
        a = jnp.dot(h, w_ref[:, :D_MODEL], preferred_element_type=F32) + b_ref[:, :D_MODEL]
        g = jnp.dot(h, w_ref[:, D_MODEL:], preferred_element_type=F32) + b_ref[:, D_MODEL:]
        o_ref[q * rb:(q + 1) * rb] = a * jax.nn.sigmoid(g)


def _conv_mix_body(up_ref, u_ref, un_ref, x_ref, wdw_ref, bdw_ref, lng_ref, lnb_ref, w2_ref, b2_ref,
                   postg_ref, gate_ref, o_ref, ue_scr, cv_scr, *, tm, n_tiles):
    i = pl.program_id(1)
    ue_scr[0:HALO_ROWS] = jnp.where(i > 0, up_ref[...], 0.0)
    ue_scr[HALO_ROWS:HALO_ROWS + tm] = u_ref[...]
    ue_scr[HALO_ROWS + tm:] = jnp.where(i < n_tiles - 1, un_ref[...], 0.0)

    nv = CONV_ROWS // V7X_SUBLANES
    sub = lax.broadcasted_iota(jnp.int32, (nv, V7X_SUBLANES, V7X_LANES), 1)
    pad = CONV_WIDTH // 2

    def conv_step(ci, carry):
        base = pl.multiple_of(ci * CONV_ROWS, CONV_ROWS)
        for lc in range(0, D_MODEL, V7X_LANES):
            lanes = slice(lc, lc + V7X_LANES)
            xe3 = ue_scr[pl.ds(base, CONV_ROWS + 2 * HALO_ROWS), lanes].reshape(nv + 4, V7X_SUBLANES, V7X_LANES)
            taps = [wdw_ref[k, :, lanes] for k in range(CONV_WIDTH)]
            y = _dwconv_rows(xe3, taps, pad, nv, sub) + bdw_ref[:, lanes]
            cv_scr[pl.ds(base, CONV_ROWS), lanes] = y.reshape(CONV_ROWS, V7X_LANES)
        return carry

    lax.fori_loop(0, tm // CONV_ROWS, conv_step, 0)

    u = cv_scr[...]
    uc = u - jnp.mean(u, axis=-1, keepdims=True)
    ln = (uc * lax.rsqrt(jnp.mean(uc * uc, axis=-1, keepdims=True) + LN_EPS)) * lng_ref[...] + lnb_ref[...]
    act = (ln * jax.nn.sigmoid(ln)).astype(BF16)
    y = jnp.dot(act, w2_ref[...], preferred_element_type=F32) + b2_ref[...]
    o_ref[...] = _post_residual(x_ref[...], y, postg_ref[...], gate_ref[...])


def _ffn_body(xp_ref, x_ref, xn_ref, gain_ref, shift_ref, scale_ref, wup_ref, wdw_ref, bdw_ref, wd_ref,
              postg_ref, gate_ref, o_ref, *, tm, n_tiles):
    i = pl.program_id(1)
    gain, shift, scale = gain_ref[...], shift_ref[...], scale_ref[...]
    x = x_ref[...]
    hp = jnp.where(i > 0, _modnorm(xp_ref[...], gain, shift, scale), 0.0).astype(BF16)
    hn = jnp.where(i < n_tiles - 1, _modnorm(xn_ref[...], gain, shift, scale), 0.0).astype(BF16)
    half = tm // 2
    h_lo = _modnorm(x[:half], gain, shift, scale).astype(BF16)
    h_hi = _modnorm(x[half:], gain, shift, scale).astype(BF16)
    h = jnp.concatenate([h_lo, h_hi], axis=0)
    he = jnp.concatenate([hp, h, hn], axis=0)

    n = tm // V7X_SUBLANES
    sub = lax.broadcasted_iota(jnp.int32, (n, V7X_SUBLANES, FFN_CHUNK), 1)
    n_chunks = D_FF // FFN_CHUNK

    def up_proj(j):
        cols = slice(j * FFN_CHUNK, (j + 1) * FFN_CHUNK)
        vcols = slice(D_FF + j * FFN_CHUNK, D_FF + (j + 1) * FFN_CHUNK)
        if j == 0:
            lo_rows = HALO_ROWS + half
            ge_lo = jnp.dot(he[:lo_rows], wup_ref[:, cols], preferred_element_type=F32)
            val_lo = jnp.dot(h_lo, wup_ref[:, vcols], preferred_element_type=F32)
            ge_hi = jnp.dot(he[lo_rows:], wup_ref[:, cols], preferred_element_type=F32)
            val_hi = jnp.dot(h_hi, wup_ref[:, vcols], preferred_element_type=F32)
            return jnp.concatenate([ge_lo, ge_hi], axis=0), jnp.concatenate([val_lo, val_hi], axis=0)
        ge = jnp.dot(he, wup_ref[:, cols], preferred_element_type=F32)
        val = jnp.dot(h, wup_ref[:, vcols], preferred_element_type=F32)
        return ge, val

    acts = []
    nxt = up_proj(0)
    for j in range(n_chunks):
        ge, val = nxt
        if j + 1 < n_chunks:
            nxt = up_proj(j + 1)
        cols = slice(j * FFN_CHUNK, (j + 1) * FFN_CHUNK)
        ge3 = ge.reshape(n + 4, V7X_SUBLANES, FFN_CHUNK)
        taps = [wdw_ref[k, :, cols] for k in range(FFN_CONV_WIDTH)]
        gc = _dwconv_rows(ge3, taps, FFN_CONV_WIDTH // 2, n, sub) + bdw_ref[:, cols]
        acts.append((_gelu_tanh(gc) * val.reshape(n, V7X_SUBLANES, FFN_CHUNK)).reshape(tm, FFN_CHUNK).astype(BF16))
    act = jnp.concatenate(acts, axis=1)

    rb = tm // FFN_TAIL_BLOCKS
    for q in range(FFN_TAIL_BLOCKS):
        rows = slice(q * rb, (q + 1) * rb)
        y = jnp.dot(act[rows], wd_ref[...], preferred_element_type=F32)
        o_ref[rows] = _post_residual(x[rows], y, postg_ref[...], gate_ref[...])


def _proj_body(x_ref, gain_ref, shift_ref, scale_ref, w_ref, *o_refs, col0):
    h = _modnorm(x_ref[...], gain_ref[...], shift_ref[...], scale_ref[...]).astype(BF16)
    for t, o_ref in enumerate(o_refs):
        cols = slice(col0 + t * D_MODEL, col0 + (t + 1) * D_MODEL)
        o_ref[...] = jnp.dot(h, w_ref[:, cols], preferred_element_type=F32).astype(o_ref.dtype)


def _bias_body(src_ref, o_ref):
    qc = lax.broadcasted_iota(jnp.int32, (GRID_W, ATTN_PAIR), 0)
    kc = lax.broadcasted_iota(jnp.int32, (GRID_W, ATTN_PAIR), 1) & (GRID_W - 1)
    c0 = jnp.clip(qc - NA_KW // 2, 0, GRID_W - NA_KW)
    in_win = (kc >= c0) & (kc < c0 + NA_KW)
    base_shift = V7X_LANES - (NA_KW - 1)
    for s in range(2 * NA_KH - 2):
        for e in range(2):
            src = jnp.broadcast_to(src_ref[s, e], (GRID_W, ATTN_PAIR))
            tile = pltpu.roll(src, base_shift, 1, stride=1, stride_axis=0)
            o_ref[s, e * GRID_W:(e + 1) * GRID_W, :] = jnp.where(in_win, tile, NEG_INF)


def _out_proj_body(a_ref, x_ref, w_ref, postg_ref, gate_ref, o_ref):
    y = jnp.dot(a_ref[...], w_ref[...], preferred_element_type=F32)
    o_ref[...] = _post_residual(x_ref[...], y, postg_ref[...], gate_ref[...])


def _attn_body(q_ref, k_ref, v_ref, kc_ref, vc_ref, bias_ref, o_ref, *, rows, rows_per_step):
    n_loc = NA_KH * GRID_W
    lo = lax.broadcasted_iota(jnp.int32, (GRID_W, ATTN_PAIR), 1) < NA_HEAD_DIM
    nt = (((1,), (1,)), ((), ()))

    def scores(p, rr):
        lanes = slice(p * ATTN_PAIR, (p + 1) * ATTN_PAIR)
        r = pl.program_id(1) * rows_per_step + rr
        r0 = jnp.clip(r - NA_KH // 2, 0, rows - NA_KH)
        row_off = r0 - r + (NA_KH - 1)
        start = pl.multiple_of(r0 * GRID_W, GRID_W)
        qp = q_ref[rr * GRID_W:(rr + 1) * GRID_W, lanes]
        zero = jnp.zeros_like(qp)
        qs = jnp.concatenate([jnp.where(lo, qp, zero), jnp.where(lo, zero, qp)], axis=0)
        bias = jnp.concatenate([bias_ref[p, row_off + 2 * jj] for jj in range(NA_KH // 2)], axis=1)
        k_all = jnp.concatenate([k_ref[pl.ds(start, n_loc), lanes], kc_ref[:, lanes]], axis=0)
        s = lax.dot_general(qs, k_all, nt, preferred_element_type=F32)
        return s[:, :n_loc] + bias, s[:, n_loc:], start

    units = [(p, rr) for p in range(NA_HEADS // 2) for rr in range(rows_per_step)]
    pending = [scores(*units[i]) for i in range(ATTN_LOOKAHEAD)]
    for idx, (p, rr) in enumerate(units):
        s_loc, s_ctx, start = pending.pop(0)
        if idx + ATTN_LOOKAHEAD < len(units):
            pending.append(scores(*units[idx + ATTN_LOOKAHEAD]))
        lanes = slice(p * ATTN_PAIR, (p + 1) * ATTN_PAIR)
        m = jnp.maximum(jnp.max(s_loc, axis=-1, keepdims=True), jnp.max(s_ctx, axis=-1, keepdims=True))
        probs = jnp.concatenate([jnp.exp(s_loc - m).astype(BF16), jnp.exp(s_ctx - m).astype(BF16)], axis=1)
        v_all = jnp.concatenate([v_ref[pl.ds(start, n_loc), lanes], vc_ref[:, lanes]], axis=0)
        v_ext = jnp.concatenate([v_all, jnp.ones_like(v_all)], axis=1)
        o_ext = jnp.dot(probs, v_ext, preferred_element_type=F32)
        o = o_ext[:, :ATTN_PAIR] / o_ext[:, ATTN_PAIR:ATTN_PAIR + 1]
        o_ref[rr * GRID_W:(rr + 1) * GRID_W, lanes] = jnp.where(lo, o[:GRID_W], o[GRID_W:]).astype(o_ref.dtype)


def _tile_rows(t, pref=512):
    return pref if t % pref == 0 else t


def _row_spec(tm, width=D_MODEL):
    return pl.BlockSpec((None, tm, width), lambda b, i: (b, i, 0))


def _halo_specs(tm, t):
    per = tm // HALO_ROWS
    last = t // HALO_ROWS - 1
    prev = pl.BlockSpec((None, HALO_ROWS, D_MODEL), lambda b, i: (b, jnp.maximum(i * per - 1, 0), 0))
    nxt = pl.BlockSpec((None, HALO_ROWS, D_MODEL), lambda b, i: (b, jnp.minimum((i + 1) * per, last), 0))
    return prev, nxt


def _const_spec(shape):
    nd = len(shape)
    return pl.BlockSpec(shape, lambda b, i: (0,) * nd, pipeline_mode=pl.Buffered(1))


def _vec_spec():
    return pl.BlockSpec((1, D_MODEL), lambda b, i: (0, 0))


def _mod_spec(layer, comp, ctx):
    if ctx is None:
        return pl.BlockSpec((None, None, None, 1, D_MODEL), lambda b, i: (layer, comp, b, 0, 0))
    return pl.BlockSpec((None, None, None, 1, D_MODEL), lambda b, i: (layer, comp, ctx, 0, 0))


def _params(nbytes):
    return pltpu.CompilerParams(dimension_semantics=("arbitrary", "arbitrary"), vmem_limit_bytes=_vmem_limit(nbytes))


def _ada_mod(cs, mod_w, mod_b):
    depth = mod_w.shape[0]
    blk = _nbytes((D_MODEL, D_MODEL), F32)
    return pl.pallas_call(
        _ada_body,
        grid=(depth, N_MOD),
        in_specs=[
            pl.BlockSpec((MOD_ROWS, D_MODEL), lambda l, k: (0, 0)),
            pl.BlockSpec((None, D_MODEL, D_MODEL), lambda l, k: (l, 0, k)),
            pl.BlockSpec((None, 1, D_MODEL), lambda l, k: (l, 0, k)),
        ],
        out_specs=pl.BlockSpec((None, None, MOD_ROWS, 1, D_MODEL), lambda l, k: (l, k, 0, 0, 0)),
        out_shape=jax.ShapeDtypeStruct((depth, N_MOD, MOD_ROWS, 1, D_MODEL), F32),
        compiler_params=pltpu.CompilerParams(dimension_semantics=("arbitrary", "arbitrary"),
                                             vmem_limit_bytes=_vmem_limit(4 * blk)),
        name="ada_mod",
    )(cs, mod_w, mod_b.reshape(depth, 1, N_MOD * D_MODEL))


def _pw1_glu(x, gain, mods, layer, ctx, w_bf, b):
    bsz, t, _ = x.shape
    tm = _tile_rows(t, PROJ_ROWS)
    est = 4 * _nbytes((tm, D_MODEL), F32) + _nbytes(w_bf.shape, BF16) + 3 * _nbytes((tm, 2 * D_MODEL), F32)
    return pl.pallas_call(
        functools.partial(_pw1_glu_body, tm=tm),
        grid=(bsz, t // tm),
        in_specs=[_row_spec(tm), _vec_spec(), _mod_spec(layer, 0, ctx), _mod_spec(layer, 1, ctx),
                  _const_spec(w_bf.shape), _const_spec(b.shape)],
        out_specs=_row_spec(tm),
        out_shape=jax.ShapeDtypeStruct((bsz, t, D_MODEL), F32),
        compiler_params=_params(est),
        name="pw1_glu",
    )(x, gain, mods, mods, w_bf, b)


def _conv_mix(u, x, wdw_b, bdw_b, ln_g, ln_b, w2_bf, b2, post_g, mods, layer, ctx):
    bsz, t, _ = x.shape
    tm = _tile_rows(t)
    n_tiles = t // tm
    prev, nxt = _halo_specs(tm, t)
    est = (8 * _nbytes((tm, D_MODEL), F32) + _nbytes(w2_bf.shape, BF16) + _nbytes(wdw_b.shape, F32)
           + 4 * _nbytes((tm, D_MODEL), F32))
    return pl.pallas_call(
        functools.partial(_conv_mix_body, tm=tm, n_tiles=n_tiles),
        grid=(bsz, n_tiles),
        in_specs=[prev, _row_spec(tm), nxt, _row_spec(tm),
                  _const_spec(wdw_b.shape), _const_spec(bdw_b.shape), _vec_spec(), _vec_spec(),
                  _const_spec(w2_bf.shape), _vec_spec(), _vec_spec(), _mod_spec(layer, 2, ctx)],
        out_specs=_row_spec(tm),
        out_shape=jax.ShapeDtypeStruct((bsz, t, D_MODEL), F32),
        scratch_shapes=[pltpu.VMEM((tm + 2 * HALO_ROWS, D_MODEL), F32), pltpu.VMEM((tm, D_MODEL), F32)],
        compiler_params=_params(est),
        name="conv_mix",
    )(u, u, u, x, wdw_b, bdw_b, ln_g, ln_b, w2_bf, b2, post_g, mods)


def _layer_spec(stacked_shape, layer):
    nd = len(stacked_shape) - 1
    return pl.BlockSpec((None,) + tuple(stacked_shape[1:]), lambda b, i: (layer,) + (0,) * nd,
                        pipeline_mode=pl.Buffered(1))


def _ffn(x, gain, post_g, mods, layer, ctx, wup, wdw_b, bdw_b, wd):
    bsz, t, _ = x.shape
    tm = _tile_rows(t, FFN_ROWS)
    n_tiles = t // tm
    prev, nxt = _halo_specs(tm, t)
    est = (8 * _nbytes((tm, D_MODEL), F32) + _nbytes(wup.shape[1:], BF16) + _nbytes(wd.shape[1:], BF16)
           + _nbytes((tm, D_FF), BF16) + 8 * _nbytes((tm + 2 * HALO_ROWS, FFN_CHUNK), F32))
    return pl.pallas_call(
        functools.partial(_ffn_body, tm=tm, n_tiles=n_tiles),
        grid=(bsz, n_tiles),
        in_specs=[prev, _row_spec(tm), nxt, _vec_spec(), _mod_spec(layer, 3, ctx), _mod_spec(layer, 4, ctx),
                  _layer_spec(wup.shape, layer), _const_spec(wdw_b.shape), _const_spec(bdw_b.shape),
                  _layer_spec(wd.shape, layer), _vec_spec(), _mod_spec(layer, 5, ctx)],
        out_specs=_row_spec(tm),
        out_shape=jax.ShapeDtypeStruct((bsz, t, D_MODEL), F32),
        compiler_params=_params(est),
        name="conv_ffn",
    )(x, x, x, gain, mods, mods, wup, wdw_b, bdw_b, wd, post_g, mods)


def _proj(x, gain, mods, layer, ctx, w_bf, col0, n_out):
    bsz, t, _ = x.shape
    tm = _tile_rows(t, PROJ_ROWS)
    est = 2 * _nbytes((tm, D_MODEL), F32) + _nbytes(w_bf.shape, BF16) + (2 * n_out + 4) * _nbytes((tm, D_MODEL), F32)
    return pl.pallas_call(
        functools.partial(_proj_body, col0=col0),
        grid=(bsz, t // tm),
        in_specs=[_row_spec(tm), _vec_spec(), _mod_spec(layer, 0, ctx), _mod_spec(layer, 1, ctx),
                  _const_spec(w_bf.shape)],
        out_specs=[_row_spec(tm)] * n_out,
        out_shape=[jax.ShapeDtypeStruct((bsz, t, D_MODEL), BF16)] * n_out,
        compiler_params=_params(est),
        name="norm_proj",
    )(x, gain, mods, mods, w_bf)


def _out_proj(a, x, w_bf, post_g, mods, layer):
    bsz, t, _ = x.shape
    tm = _tile_rows(t)
    est = 6 * _nbytes((tm, D_MODEL), F32) + _nbytes(w_bf.shape, BF16)
    return pl.pallas_call(
        _out_proj_body,
        grid=(bsz, t // tm),
        in_specs=[_row_spec(tm), _row_spec(tm), _const_spec(w_bf.shape), _vec_spec(), _mod_spec(layer, 2, None)],
        out_specs=_row_spec(tm),
        out_shape=jax.ShapeDtypeStruct((bsz, t, D_MODEL), F32),
        compiler_params=_params(est),
        name="out_proj",
    )(a, x, w_bf, post_g, mods)


def _attention(q, k, v, kc, vc, bias):
    bsz, t, _ = q.shape
    rows = t // GRID_W
    c_len = kc.shape[1]
    rps = ATTN_ROWS_PER_STEP
    assert rows % rps == 0
    tq = rps * GRID_W
    est = (4 * _nbytes((t, D_MODEL), BF16) + 4 * _nbytes((c_len, D_MODEL), BF16)
           + _nbytes(bias.shape, F32) + 4 * _nbytes((tq, D_MODEL), BF16))
    return pl.pallas_call(
        functools.partial(_attn_body, rows=rows, rows_per_step=rps),
        grid=(bsz, rows // rps),
        in_specs=[
            pl.BlockSpec((None, tq, D_MODEL), lambda b, r: (b, r, 0)),
            pl.BlockSpec((None, t, D_MODEL), lambda b, r: (b, 0, 0)),
            pl.BlockSpec((None, t, D_MODEL), lambda b, r: (b, 0, 0)),
            pl.BlockSpec((None, c_len, D_MODEL), lambda b, r: (b, 0, 0)),
            pl.BlockSpec((None, c_len, D_MODEL), lambda b, r: (b, 0, 0)),
            _const_spec(bias.shape),
        ],
        out_specs=pl.BlockSpec((None, tq, D_MODEL), lambda b, r: (b, r, 0)),
        out_shape=jax.ShapeDtypeStruct((bsz, t, D_MODEL), BF16),
        compiler_params=_params(est),
        name="na_attention",
    )(q, k, v, kc, vc, bias)


def _attn_bias_table(rpb):
    n_pairs = NA_HEADS // 2
    n_s = 2 * NA_KH - 2
    rp = jnp.pad(rpb.astype(F32), ((0, 0), (0, 0), (0, GRID_W - (2 * NA_KW - 1))))
    src = jnp.concatenate([rp[:, :n_s], rp[:, 1:]], axis=-1)
    src = jnp.transpose(src.reshape(n_pairs, 2, n_s, ATTN_PAIR), (0, 2, 1, 3))[:, :, :, None, :]
    return pl.pallas_call(
        _bias_body,
        grid=(n_pairs,),
        in_specs=[pl.BlockSpec((None, n_s, 2, 1, ATTN_PAIR), lambda p: (p, 0, 0, 0, 0))],
        out_specs=pl.BlockSpec((None, n_s, 2 * GRID_W, ATTN_PAIR), lambda p: (p, 0, 0, 0)),
        out_shape=jax.ShapeDtypeStruct((n_pairs, n_s, 2 * GRID_W, ATTN_PAIR), F32),
        name="na_bias",
    )(src)


def _ffn_conv_params(w_dw, b_dw):
    wdw_b = jnp.broadcast_to(w_dw[:, None, :], (FFN_CONV_WIDTH, V7X_SUBLANES, D_FF))
    bdw_b = jnp.broadcast_to(b_dw[None, :], (V7X_SUBLANES, D_FF))
    return wdw_b, bdw_b


def kernel(x, c, ctx, c_ctx, mod_w, mod_b, mix_pre_g, mix_post_g, ffn_pre_g, ffn_post_g, cv_w_pw1, cv_b_pw1, cv_w_dw, cv_b_dw, cv_ln_g, cv_ln_b, cv_w_pw2, cv_b_pw2, na_w_qkv, na_w_o, na_rpb, ffn_w_up, ffn_w_dw, ffn_b_dw, ffn_w_down):
    bsz = x.shape[0]
    assert bsz + 1 <= MOD_ROWS and x.shape[2] == D_MODEL and mod_w.shape[0] == DEPTH
    ctx_row = bsz
    cs = jnp.concatenate([c, c_ctx[None, :], jnp.zeros((MOD_ROWS - bsz - 1, D_MODEL), F32)], axis=0)
    mods = _ada_mod(cs, mod_w, mod_b)
    wup_all = ffn_w_up.astype(BF16)
    wd_all = ffn_w_down.astype(BF16)

    h_ctx = ctx
    for i in range(DEPTH):
        last = i == DEPTH - 1
        j = i // 2
        use_na = (i % 2) == 1
        pre_g = mix_pre_g[i][None, :]
        post_g = mix_post_g[i][None, :]
        wdw3, bdw3 = _ffn_conv_params(ffn_w_dw[i], ffn_b_dw[i])
        ffn = (wup_all, wdw3, bdw3, wd_all)
        if use_na:
            scale = NA_HEAD_DIM ** -0.5
            col_scale = jnp.concatenate([jnp.full((D_MODEL,), scale, F32), jnp.ones((2 * D_MODEL,), F32)])
            w_qkv = (na_w_qkv[j] * col_scale[None, :]).astype(BF16)
            q, k, v = _proj(x, pre_g, mods, i, None, w_qkv, 0, 3)
            kc, vc = _proj(h_ctx, pre_g, mods, i, ctx_row, w_qkv, D_MODEL, 2)
            attn = _attention(q, k, v, kc, vc, _attn_bias_table(na_rpb[j]))
            x = _out_proj(attn, x, na_w_o[j].astype(BF16), post_g, mods, i)
            assert last, "context queries are only needed when another layer follows"
        else:
            w1 = cv_w_pw1[j].astype(BF16)
            b1 = cv_b_pw1[j][None, :]
            wdw_b = jnp.broadcast_to(cv_w_dw[j][:, None, :], (CONV_WIDTH, V7X_SUBLANES, D_MODEL))
            bdw_b = jnp.broadcast_to(cv_b_dw[j][None, :], (V7X_SUBLANES, D_MODEL))
            w2 = cv_w_pw2[j].astype(BF16)
            cv = (wdw_b, bdw_b, cv_ln_g[j][None, :], cv_ln_b[j][None, :], w2, cv_b_pw2[j][None, :], post_g)
            u = _pw1_glu(x, pre_g, mods, i, None, w1, b1)
            x = _conv_mix(u, x, *cv, mods, i, None)
            if not last:
                u_ctx = _pw1_glu(h_ctx, pre_g, mods, i, ctx_row, w1, b1)
                h_ctx = _conv_mix(u_ctx, h_ctx, *cv, mods, i, ctx_row)
        x = _ffn(x, ffn_pre_g[i][None, :], ffn_post_g[i][None, :], mods, i, None, *ffn)
        if not last:
            h_ctx = _ffn(h_ctx, ffn_pre_g[i][None, :], ffn_post_g[i][None, :], mods, i, ctx_row, *ffn)
    return x
```

```python
import functools
import math

import jax
import jax.numpy as jnp
from jax import lax
from jax.experimental import pallas as pl
from jax.experimental.pallas import tpu as pltpu

D_MODEL = 1024
DEPTH = 2
GRID_W = 64
CONV_WIDTH = 31
NA_HEADS = 16
NA_HEAD_DIM = D_MODEL // NA_HEADS
NA_KH = 8
NA_KW = 16
D_FF = 2816
FFN_CONV_WIDTH = 3
N_MOD = 6
RMS_EPS = 1e-6
LN_EPS = 1e-5
NEG_INF = -1e30

V7X_LANES = 128
V7X_SUBLANES = 8
V7X_VMEM_BYTES = 64 * 1024 * 1024

HALO_ROWS = 2 * V7X_SUBLANES
MOD_ROWS = 16
FFN_CHUNK = 256
FFN_ROWS = 512
FFN_TAIL_BLOCKS = 2
CONV_ROWS = 128
PROJ_ROWS = 1024
ATTN_PAIR = 2 * NA_HEAD_DIM
ATTN_ROWS_PER_STEP = 4
ATTN_LOOKAHEAD = 3

F32 = jnp.float32
BF16 = jnp.bfloat16


def _vmem_limit(nbytes):
    return int(min(V7X_VMEM_BYTES - 6 * 1024 * 1024, max(2 * nbytes, 24 * 1024 * 1024)))


def _nbytes(shape, dtype):
    return math.prod(shape) * jnp.dtype(dtype).itemsize


def _modnorm(xf, gain, shift, scale):
    ms = jnp.mean(xf * xf, axis=-1, keepdims=True)
    return (xf * lax.rsqrt(ms + RMS_EPS)) * (gain * (1.0 + scale)) + shift


def _post_residual(xf, y, post_g, gate):
    ms = jnp.mean(y * y, axis=-1, keepdims=True)
    return xf + gate * ((y * lax.rsqrt(ms + RMS_EPS)) * post_g)


def _gelu_tanh(x):
    c = math.sqrt(2.0 / math.pi)
    return x * (0.5 * (1.0 + jnp.tanh(c * (x + 0.044715 * (x * x * x)))))


def _dwconv_rows(xe3, taps, pad, n, sub):
    out = None
    for r in range(V7X_SUBLANES):
        offs = [d for d in range(-pad, pad + 1) if d % V7X_SUBLANES == r]
        if not offs:
            continue
        rr = xe3 if r == 0 else pltpu.roll(xe3, V7X_SUBLANES - r, axis=1)
        m = n if r == 0 else n + 1
        part = None
        for d in offs:
            a = d // V7X_SUBLANES
            term = taps[d + pad] * rr[2 + a: 2 + a + m]
            part = term if part is None else part + term
        if r != 0:
            part = jnp.where(sub < V7X_SUBLANES - r, part[0:n], part[1:n + 1])
        out = part if out is None else out + part
    return out


def _ada_body(cs_ref, w_ref, b_ref, o_ref):
    s = cs_ref[...]
    s = s * jax.nn.sigmoid(s)
    m = jnp.dot(s.astype(BF16), w_ref[...].astype(BF16), preferred_element_type=F32) + b_ref[...]
    for row in range(MOD_ROWS):
        o_ref[row] = m[row:row + 1, :]


def _pw1_glu_body(x_ref, gain_ref, shift_ref, scale_ref, w_ref, b_ref, o_ref, *, tm):
    gain, shift, scale = gain_ref[...], shift_ref[...], scale_ref[...]
    rb = tm // 2
    hs = [_modnorm(x_ref[q * rb:(q + 1) * rb], gain, shift, scale).astype(BF16) for q in range(2)]
    for q, h in enumerate(hs):
        a = jnp.dot(h, w_ref[:, :D_MODEL], preferred_element_type=F32) + b_ref[:, :D_MODEL]
        g = jnp.dot(h, w_ref[:, D_MODEL:], preferred_element_type=F32) + b_ref[:, D_MODEL:]
        o_ref[q * rb:(q + 1) * rb] = a * jax.nn.sigmoid(g)


def _conv_mix_body(up_ref, u_ref, un_ref, x_ref, wdw_ref, bdw_ref, lng_ref, lnb_ref, w2_ref, b2_ref,
                   postg_ref, gate_ref, o_ref, ue_scr, cv_scr, *, tm, n_tiles):
    i = pl.program_id(1)
    for s in range(D_MODEL // V7X_LANES):
        lanes = slice(s * V7X_LANES, (s + 1) * V7X_LANES)
        ue_scr[s, 0:HALO_ROWS] = jnp.where(i > 0, up_ref[:, lanes], 0.0)
        ue_scr[s, HALO_ROWS:HALO_ROWS + tm] = u_ref[:, lanes]
        ue_scr[s, HALO_ROWS + tm:] = jnp.where(i < n_tiles - 1, un_ref[:, lanes], 0.0)

    first = HALO_ROWS - CONV_WIDTH // 2

    def conv_step(ci, carry):
        base = pl.multiple_of(ci * CONV_ROWS, CONV_ROWS)
        for s in range(D_MODEL // V7X_LANES):
            lanes = slice(s * V7X_LANES, (s + 1) * V7X_LANES)
            acc = None
            for k in range(CONV_WIDTH):
                shifted = ue_scr[s, pl.ds(base + first + k, CONV_ROWS, stride=1), :]
                term = wdw_ref[k, 0:1, lanes] * shifted
                acc = term if acc is None else acc + term
            cv_scr[pl.ds(base, CONV_ROWS), lanes] = acc + bdw_ref[0:1, lanes]
        return carry

    lax.fori_loop(0, tm // CONV_ROWS, conv_step, 0)

    u = cv_scr[...]
    uc = u - jnp.mean(u, axis=-1, keepdims=True)
    ln = (uc * lax.rsqrt(jnp.mean(uc * uc, axis=-1, keepdims=True) + LN_EPS)) * lng_ref[...] + lnb_ref[...]
    act = (ln * jax.nn.sigmoid(ln)).astype(BF16)
    y = jnp.dot(act, w2_ref[...], preferred_element_type=F32) + b2_ref[...]
    o_ref[...] = _post_residual(x_ref[...], y, postg_ref[...], gate_ref[...])


def _ffn_body(xp_ref, x_ref, xn_ref, gain_ref, shift_ref, scale_ref, wup_ref, wdw_ref, bdw_ref, wd_ref,
              postg_ref, gate_ref, o_ref, *, tm, n_tiles):
    i = pl.program_id(1)
    gain, shift, scale = gain_ref[...], shift_ref[...], scale_ref[...]
    x = x_ref[...]
    hp = jnp.where(i > 0, _modnorm(xp_ref[...], gain, shift, scale), 0.0).astype(BF16)
    hn = jnp.where(i < n_tiles - 1, _modnorm(xn_ref[...], gain, shift, scale), 0.0).astype(BF16)
    half = tm // 2
    h_lo = _modnorm(x[:half], gain, shift, scale).astype(BF16)
    h_hi = _modnorm(x[half:], gain, shift, scale).astype(BF16)
    h = jnp.concatenate([h_lo, h_hi], axis=0)
    he = jnp.concatenate([hp, h, hn], axis=0)

    n = tm // V7X_SUBLANES
    sub = lax.broadcasted_iota(jnp.int32, (n, V7X_SUBLANES, FFN_CHUNK), 1)
    n_chunks = D_FF // FFN_CHUNK

    def up_proj(j):
        cols = slice(j * FFN_CHUNK, (j + 1) * FFN_CHUNK)
        vcols = slice(D_FF + j * FFN_CHUNK, D_FF + (j + 1) * FFN_CHUNK)
        if j == 0:
            lo_rows = HALO_ROWS + half
            ge_lo = jnp.dot(he[:lo_rows], wup_ref[:, cols], preferred_element_type=F32)
            val_lo = jnp.dot(h_lo, wup_ref[:, vcols], preferred_element_type=F32)
            ge_hi = jnp.dot(he[lo_rows:], wup_ref[:, cols], preferred_element_type=F32)
            val_hi = jnp.dot(h_hi, wup_ref[:, vcols], preferred_element_type=F32)
            return jnp.concatenate([ge_lo, ge_hi], axis=0), jnp.concatenate([val_lo, val_hi], axis=0)
        ge = jnp.dot(he, wup_ref[:, cols], preferred_element_type=F32)
        val = jnp.dot(h, wup_ref[:, vcols], preferred_element_type=F32)
        return ge, val

    acts = []
    nxt = up_proj(0)
    for j in range(n_chunks):
        ge, val = nxt
        if j + 1 < n_chunks:
            nxt = up_proj(j + 1)
        cols = slice(j * FFN_CHUNK, (j + 1) * FFN_CHUNK)
        ge3 = ge.reshape(n + 4, V7X_SUBLANES, FFN_CHUNK)
        taps = [wdw_ref[k, :, cols] for k in range(FFN_CONV_WIDTH)]
        gc = _dwconv_rows(ge3, taps, FFN_CONV_WIDTH // 2, n, sub) + bdw_ref[:, cols]
        acts.append((_gelu_tanh(gc) * val.reshape(n, V7X_SUBLANES, FFN_CHUNK)).reshape(tm, FFN_CHUNK).astype(BF16))
    act = jnp.concatenate(acts, axis=1)

    rb = tm // FFN_TAIL_BLOCKS
    for q in range(FFN_TAIL_BLOCKS):
        rows = slice(q * rb, (q + 1) * rb)
        y = jnp.dot(act[rows], wd_ref[...], preferred_element_type=F32)
        o_ref[rows] = _post_residual(x[rows], y, postg_ref[...], gate_ref[...])


def _proj_body(x_ref, gain_ref, shift_ref, scale_ref, w_ref, *o_refs, col0):
    h = _modnorm(x_ref[...], gain_ref[...], shift_ref[...], scale_ref[...]).astype(BF16)
    for t, o_ref in enumerate(o_refs):
        cols = slice(col0 + t * D_MODEL, col0 + (t + 1) * D_MODEL)
        o_ref[...] = jnp.dot(h, w_ref[:, cols], preferred_element_type=F32).astype(o_ref.dtype)


def _bias_body(src_ref, o_ref):
    qc = lax.broadcasted_iota(jnp.int32, (GRID_W, ATTN_PAIR), 0)
    kc = lax.broadcasted_iota(jnp.int32, (GRID_W, ATTN_PAIR), 1) & (GRID_W - 1)
    c0 = jnp.clip(qc - NA_KW // 2, 0, GRID_W - NA_KW)
    in_win = (kc >= c0) & (kc < c0 + NA_KW)
    base_shift = V7X_LANES - (NA_KW - 1)
    for s in range(2 * NA_KH - 2):
        for e in range(2):
            src = jnp.broadcast_to(src_ref[s, e], (GRID_W, ATTN_PAIR))
            tile = pltpu.roll(src, base_shift, 1, stride=1, stride_axis=0)
            o_ref[s, e * GRID_W:(e + 1) * GRID_W, :] = jnp.where(in_win, tile, NEG_INF)


def _out_proj_body(a_ref, x_ref, w_ref, postg_ref, gate_ref, o_ref):
    y = jnp.dot(a_ref[...], w_ref[...], preferred_element_type=F32)
    o_ref[...] = _post_residual(x_ref[...], y, postg_ref[...], gate_ref[...])


def _attn_body(q_ref, k_ref, v_ref, kc_ref, vc_ref, bias_ref, o_ref, *, rows, rows_per_step):
    n_loc = NA_KH * GRID_W
    lo = lax.broadcasted_iota(jnp.int32, (GRID_W, ATTN_PAIR), 1) < NA_HEAD_DIM
    nt = (((1,), (1,)), ((), ()))

    def scores(p, rr):
        lanes = slice(p * ATTN_PAIR, (p + 1) * ATTN_PAIR)
        r = pl.program_id(1) * rows_per_step + rr
        r0 = jnp.clip(r - NA_KH // 2, 0, rows - NA_KH)
        row_off = r0 - r + (NA_KH - 1)
        start = pl.multiple_of(r0 * GRID_W, GRID_W)
        qp = q_ref[rr * GRID_W:(rr + 1) * GRID_W, lanes]
        zero = jnp.zeros_like(qp)
        qs = jnp.concatenate([jnp.where(lo, qp, zero), jnp.where(lo, zero, qp)], axis=0)
        bias = jnp.concatenate([bias_ref[p, row_off + 2 * jj] for jj in range(NA_KH // 2)], axis=1)
        k_all = jnp.concatenate([k_ref[pl.ds(start, n_loc), lanes], kc_ref[:, lanes]], axis=0)
        s = lax.dot_general(qs, k_all, nt, preferred_element_type=F32)
        return s[:, :n_loc] + bias, s[:, n_loc:], start

    units = [(p, rr) for p in range(NA_HEADS // 2) for rr in range(rows_per_step)]
    pending = [scores(*units[i]) for i in range(ATTN_LOOKAHEAD)]
    for idx, (p, rr) in enumerate(units):
        s_loc, s_ctx, start = pending.pop(0)
        if idx + ATTN_LOOKAHEAD < len(units):
            pending.append(scores(*units[idx + ATTN_LOOKAHEAD]))
        lanes = slice(p * ATTN_PAIR, (p + 1) * ATTN_PAIR)
        m = jnp.maximum(jnp.max(s_loc, axis=-1, keepdims=True), jnp.max(s_ctx, axis=-1, keepdims=True))
        probs = jnp.concatenate([jnp.exp(s_loc - m).astype(BF16), jnp.exp(s_ctx - m).astype(BF16)], axis=1)
        v_all = jnp.concatenate([v_ref[pl.ds(start, n_loc), lanes], vc_ref[:, lanes]], axis=0)
        v_ext = jnp.concatenate([v_all, jnp.ones_like(v_all)], axis=1)
        o_ext = jnp.dot(probs, v_ext, preferred_element_type=F32)
        o = o_ext[:, :ATTN_PAIR] / o_ext[:, ATTN_PAIR:ATTN_PAIR + 1]
        o_ref[rr * GRID_W:(rr + 1) * GRID_W, lanes] = jnp.where(lo, o[:GRID_W], o[GRID_W:]).astype(o_ref.dtype)


def _tile_rows(t, pref=512):
    return pref if t % pref == 0 else t


def _row_spec(tm, width=D_MODEL):
    return pl.BlockSpec((None, tm, width), lambda b, i: (b, i, 0))


def _halo_specs(tm, t):
    per = tm // HALO_ROWS
    last = t // HALO_ROWS - 1
    prev = pl.BlockSpec((None, HALO_ROWS, D_MODEL), lambda b, i: (b, jnp.maximum(i * per - 1, 0), 0))
    nxt = pl.BlockSpec((None, HALO_ROWS, D_MODEL), lambda b, i: (b, jnp.minimum((i + 1) * per, last), 0))
    return prev, nxt


def _const_spec(shape):
    nd = len(shape)
    return pl.BlockSpec(shape, lambda b, i: (0,) * nd, pipeline_mode=pl.Buffered(1))


def _vec_spec():
    return pl.BlockSpec((1, D_MODEL), lambda b, i: (0, 0))


def _mod_spec(layer, comp, ctx):
    if ctx is None:
        return pl.BlockSpec((None, None, None, 1, D_MODEL), lambda b, i: (layer, comp, b, 0, 0))
    return pl.BlockSpec((None, None, None, 1, D_MODEL), lambda b, i: (layer, comp, ctx, 0, 0))


def _params(nbytes):
    return pltpu.CompilerParams(dimension_semantics=("arbitrary", "arbitrary"), vmem_limit_bytes=_vmem_limit(nbytes))


def _ada_mod(cs, mod_w, mod_b):
    depth = mod_w.shape[0]
    blk = _nbytes((D_MODEL, D_MODEL), F32)
    return pl.pallas_call(
        _ada_body,
        grid=(depth, N_MOD),
        in_specs=[
            pl.BlockSpec((MOD_ROWS, D_MODEL), lambda l, k: (0, 0)),
            pl.BlockSpec((None, D_MODEL, D_MODEL), lambda l, k: (l, 0, k)),
            pl.BlockSpec((None, 1, D_MODEL), lambda l, k: (l, 0, k)),
        ],
        out_specs=pl.BlockSpec((None, None, MOD_ROWS, 1, D_MODEL), lambda l, k: (l, k, 0, 0, 0)),
        out_shape=jax.ShapeDtypeStruct((depth, N_MOD, MOD_ROWS, 1, D_MODEL), F32),
        compiler_params=pltpu.CompilerParams(dimension_semantics=("arbitrary", "arbitrary"),
                                             vmem_limit_bytes=_vmem_limit(4 * blk)),
        name="ada_mod",
    )(cs, mod_w, mod_b.reshape(depth, 1, N_MOD * D_MODEL))


def _pw1_glu(x, gain, mods, layer, ctx, w_bf, b):
    bsz, t, _ = x.shape
    tm = _tile_rows(t, PROJ_ROWS)
    est = 4 * _nbytes((tm, D_MODEL), F32) + _nbytes(w_bf.shape, BF16) + 3 * _nbytes((tm, 2 * D_MODEL), F32)
    return pl.pallas_call(
        functools.partial(_pw1_glu_body, tm=tm),
        grid=(bsz, t // tm),
        in_specs=[_row_spec(tm), _vec_spec(), _mod_spec(layer, 0, ctx), _mod_spec(layer, 1, ctx),
                  _const_spec(w_bf.shape), _const_spec(b.shape)],
        out_specs=_row_spec(tm),
        out_shape=jax.ShapeDtypeStruct((bsz, t, D_MODEL), F32),
        compiler_params=_params(est),
        name="pw1_glu",
    )(x, gain, mods, mods, w_bf, b)


def _conv_mix(u, x, wdw_b, bdw_b, ln_g, ln_b, w2_bf, b2, post_g, mods, layer, ctx):
    bsz, t, _ = x.shape
    tm = _tile_rows(t)
    n_tiles = t // tm
    prev, nxt = _halo_specs(tm, t)
    est = (8 * _nbytes((tm, D_MODEL), F32) + _nbytes(w2_bf.shape, BF16) + _nbytes(wdw_b.shape, F32)
           + 4 * _nbytes((tm, D_MODEL), F32))
    return pl.pallas_call(
        functools.partial(_conv_mix_body, tm=tm, n_tiles=n_tiles),
        grid=(bsz, n_tiles),
        in_specs=[prev, _row_spec(tm), nxt, _row_spec(tm),
                  _const_spec(wdw_b.shape), _const_spec(bdw_b.shape), _vec_spec(), _vec_spec(),
                  _const_spec(w2_bf.shape), _vec_spec(), _vec_spec(), _mod_spec(layer, 2, ctx)],
        out_specs=_row_spec(tm),
        out_shape=jax.ShapeDtypeStruct((bsz, t, D_MODEL), F32),
        scratch_shapes=[pltpu.VMEM((D_MODEL // V7X_LANES, tm + 2 * HALO_ROWS, V7X_LANES), F32),
                        pltpu.VMEM((tm, D_MODEL), F32)],
        compiler_params=_params(est),
        name="conv_mix",
    )(u, u, u, x, wdw_b, bdw_b, ln_g, ln_b, w2_bf, b2, post_g, mods)


def _layer_spec(stacked_shape, layer):
    nd = len(stacked_shape) - 1
    return pl.BlockSpec((None,) + tuple(stacked_shape[1:]), lambda b, i: (layer,) + (0,) * nd,
                        pipeline_mode=pl.Buffered(1))


def _ffn(x, gain, post_g, mods, layer, ctx, wup, wdw_b, bdw_b, wd):
    bsz, t, _ = x.shape
    tm = _tile_rows(t, FFN_ROWS)
    n_tiles = t // tm
    prev, nxt = _halo_specs(tm, t)
    est = (8 * _nbytes((tm, D_MODEL), F32) + _nbytes(wup.shape[1:], BF16) + _nbytes(wd.shape[1:], BF16)
           + _nbytes((tm, D_FF), BF16) + 8 * _nbytes((tm + 2 * HALO_ROWS, FFN_CHUNK), F32))
    return pl.pallas_call(
        functools.partial(_ffn_body, tm=tm, n_tiles=n_tiles),
        grid=(bsz, n_tiles),
        in_specs=[prev, _row_spec(tm), nxt, _vec_spec(), _mod_spec(layer, 3, ctx), _mod_spec(layer, 4, ctx),
                  _layer_spec(wup.shape, layer), _const_spec(wdw_b.shape), _const_spec(bdw_b.shape),
                  _layer_spec(wd.shape, layer), _vec_spec(), _mod_spec(layer, 5, ctx)],
        out_specs=_row_spec(tm),
        out_shape=jax.ShapeDtypeStruct((bsz, t, D_MODEL), F32),
        compiler_params=_params(est),
        name="conv_ffn",
    )(x, x, x, gain, mods, mods, wup, wdw_b, bdw_b, wd, post_g, mods)


def _proj(x, gain, mods, layer, ctx, w_bf, col0, n_out):
    bsz, t, _ = x.shape
    tm = _tile_rows(t, PROJ_ROWS)
    est = 2 * _nbytes((tm, D_MODEL), F32) + _nbytes(w_bf.shape, BF16) + (2 * n_out + 4) * _nbytes((tm, D_MODEL), F32)
    return pl.pallas_call(
        functools.partial(_proj_body, col0=col0),
        grid=(bsz, t // tm),
        in_specs=[_row_spec(tm), _vec_spec(), _mod_spec(layer, 0, ctx), _mod_spec(layer, 1, ctx),
                  _const_spec(w_bf.shape)],
        out_specs=[_row_spec(tm)] * n_out,
        out_shape=[jax.ShapeDtypeStruct((bsz, t, D_MODEL), BF16)] * n_out,
        compiler_params=_params(est),
        name="norm_proj",
    )(x, gain, mods, mods, w_bf)


def _out_proj(a, x, w_bf, post_g, mods, layer):
    bsz, t, _ = x.shape
    tm = _tile_rows(t)
    est = 6 * _nbytes((tm, D_MODEL), F32) + _nbytes(w_bf.shape, BF16)
    return pl.pallas_call(
        _out_proj_body,
        grid=(bsz, t // tm),
        in_specs=[_row_spec(tm), _row_spec(tm), _const_spec(w_bf.shape), _vec_spec(), _mod_spec(layer, 2, None)],
        out_specs=_row_spec(tm),
        out_shape=jax.ShapeDtypeStruct((bsz, t, D_MODEL), F32),
        compiler_params=_params(est),
        name="out_proj",
    )(a, x, w_bf, post_g, mods)


def _attention(q, k, v, kc, vc, bias):
    bsz, t, _ = q.shape
    rows = t // GRID_W
    c_len = kc.shape[1]
    rps = ATTN_ROWS_PER_STEP
    assert rows % rps == 0
    tq = rps * GRID_W
    est = (4 * _nbytes((t, D_MODEL), BF16) + 4 * _nbytes((c_len, D_MODEL), BF16)
           + _nbytes(bias.shape, F32) + 4 * _nbytes((tq, D_MODEL), BF16))
    return pl.pallas_call(
        functools.partial(_attn_body, rows=rows, rows_per_step=rps),
        grid=(bsz, rows // rps),
        in_specs=[
            pl.BlockSpec((None, tq, D_MODEL), lambda b, r: (b, r, 0)),
            pl.BlockSpec((None, t, D_MODEL), lambda b, r: (b, 0, 0)),
            pl.BlockSpec((None, t, D_MODEL), lambda b, r: (b, 0, 0)),
            pl.BlockSpec((None, c_len, D_MODEL), lambda b, r: (b, 0, 0)),
            pl.BlockSpec((None, c_len, D_MODEL), lambda b, r: (b, 0, 0)),
            _const_spec(bias.shape),
        ],
        out_specs=pl.BlockSpec((None, tq, D_MODEL), lambda b, r: (b, r, 0)),
        out_shape=jax.ShapeDtypeStruct((bsz, t, D_MODEL), BF16),
        compiler_params=_params(est),
        name="na_attention",
    )(q, k, v, kc, vc, bias)


def _attn_bias_table(rpb):
    n_pairs = NA_HEADS // 2
    n_s = 2 * NA_KH - 2
    rp = jnp.pad(rpb.astype(F32), ((0, 0), (0, 0), (0, GRID_W - (2 * NA_KW - 1))))
    src = jnp.concatenate([rp[:, :n_s], rp[:, 1:]], axis=-1)
    src = jnp.transpose(src.reshape(n_pairs, 2, n_s, ATTN_PAIR), (0, 2, 1, 3))[:, :, :, None, :]
    return pl.pallas_call(
        _bias_body,
        grid=(n_pairs,),
        in_specs=[pl.BlockSpec((None, n_s, 2, 1, ATTN_PAIR), lambda p: (p, 0, 0, 0, 0))],
        out_specs=pl.BlockSpec((None, n_s, 2 * GRID_W, ATTN_PAIR), lambda p: (p, 0, 0, 0)),
        out_shape=jax.ShapeDtypeStruct((n_pairs, n_s, 2 * GRID_W, ATTN_PAIR), F32),
        name="na_bias",
    )(src)


def _ffn_conv_params(w_dw, b_dw):
    wdw_b = jnp.broadcast_to(w_dw[:, None, :], (FFN_CONV_WIDTH, V7X_SUBLANES, D_FF))
    bdw_b = jnp.broadcast_to(b_dw[None, :], (V7X_SUBLANES, D_FF))
    return wdw_b, bdw_b


def kernel(x, c, ctx, c_ctx, mod_w, mod_b, mix_pre_g, mix_post_g, ffn_pre_g, ffn_post_g, cv_w_pw1, cv_b_pw1, cv_w_dw, cv_b_dw, cv_ln_g, cv_ln_b, cv_w_pw2, cv_b_pw2, na_w_qkv, na_w_o, na_rpb, ffn_w_up, ffn_w_dw, ffn_b_dw, ffn_w_down):
    bsz = x.shape[0]
    assert bsz + 1 <= MOD_ROWS and x.shape[2] == D_MODEL and mod_w.shape[0] == DEPTH
    ctx_row = bsz
    cs = jnp.concatenate([c, c_ctx[None, :], jnp.zeros((MOD_ROWS - bsz - 1, D_MODEL), F32)], axis=0)
    mods = _ada_mod(cs, mod_w, mod_b)
    wup_all = ffn_w_up.astype(BF16)
    wd_all = ffn_w_down.astype(BF16)

    h_ctx = ctx
    for i in range(DEPTH):
        last = i == DEPTH - 1
        j = i // 2
        use_na = (i % 2) == 1
        pre_g = mix_pre_g[i][None, :]
        post_g = mix_post_g[i][None, :]
        wdw3, bdw3 = _ffn_conv_params(ffn_w_dw[i], ffn_b_dw[i])
        ffn = (wup_all, wdw3, bdw3, wd_all)
        if use_na:
            scale = NA_HEAD_DIM ** -0.5
            col_scale = jnp.concatenate([jnp.full((D_MODEL,), scale, F32), jnp.ones((2 * D_MODEL,), F32)])
            w_qkv = (na_w_qkv[j] * col_scale[None, :]).astype(BF16)
            q, k, v = _proj(x, pre_g, mods, i, None, w_qkv, 0, 3)
            kc, vc = _proj(h_ctx, pre_g, mods, i, ctx_row, w_qkv, D_MODEL, 2)
            attn = _attention(q, k, v, kc, vc, _attn_bias_table(na_rpb[j]))
            x = _out_proj(attn, x, na_w_o[j].astype(BF16), post_g, mods, i)
            assert last, "context queries are only needed when another layer follows"
        else:
            w1 = cv_w_pw1[j].astype(BF16)
            b1 = cv_b_pw1[j][None, :]
            wdw_b = jnp.broadcast_to(cv_w_dw[j][:, None, :], (CONV_WIDTH, V7X_SUBLANES, D_MODEL))
            bdw_b = jnp.broadcast_to(cv_b_dw[j][None, :], (V7X_SUBLANES, D_MODEL))
            w2 = cv_w_pw2[j].astype(BF16)
            cv = (wdw_b, bdw_b, cv_ln_g[j][None, :], cv_ln_b[j][None, :], w2, cv_b_pw2[j][None, :], post_g)
            u = _pw1_glu(x, pre_g, mods, i, None, w1, b1)
            x = _conv_mix(u, x, *cv, mods, i, None)
            if not last:
                u_ctx = _pw1_glu(h_ctx, pre_g, mods, i, ctx_row, w1, b1)
                h_ctx = _conv_mix(u_ctx, h_ctx, *cv, mods, i, ctx_row)
        x = _ffn(x, ffn_pre_g[i][None, :], ffn_post_g[i][None, :], mods, i, None, *ffn)
        if not last:
            h_ctx = _ffn(h_ctx, ffn_pre_g[i][None, :], ffn_post_g[i][None, :], mods, i, ctx_row, *ffn)
    return x
```

```python
import functools
import math

import jax
import jax.numpy as jnp
from jax import lax
from jax.experimental import pallas as pl
from jax.experimental.pallas import tpu as pltpu

D_MODEL = 1024
DEPTH = 2
GRID_W = 64
CONV_WIDTH = 31
NA_HEADS = 16
NA_HEAD_DIM = D_MODEL // NA_HEADS
NA_KH = 8
NA_KW = 16
D_FF = 2816
FFN_CONV_WIDTH = 3
N_MOD = 6
RMS_EPS = 1e-6
LN_EPS = 1e-5
NEG_INF = -1e30

V7X_LANES = 128
V7X_SUBLANES = 8
V7X_VMEM_BYTES = 64 * 1024 * 1024

HALO_ROWS = 2 * V7X_SUBLANES
MOD_ROWS = 16
FFN_CHUNK = 256
FFN_ROWS = 512
FFN_TAIL_BLOCKS = 2
CONV_ROWS = 128
PROJ_ROWS = 1024
ATTN_PAIR = 2 * NA_HEAD_DIM
ATTN_ROWS_PER_STEP = 4
ATTN_LOOKAHEAD = 3

F32 = jnp.float32
BF16 = jnp.bfloat16


def _vmem_limit(nbytes):
    return int(min(V7X_VMEM_BYTES - 6 * 1024 * 1024, max(2 * nbytes, 24 * 1024 * 1024)))


def _nbytes(shape, dtype):
    return math.prod(shape) * jnp.dtype(dtype).itemsize


def _modnorm(xf, gain, shift, scale):
    ms = jnp.mean(xf * xf, axis=-1, keepdims=True)
    return (xf * lax.rsqrt(ms + RMS_EPS)) * (gain * (1.0 + scale)) + shift


def _post_residual(xf, y, post_g, gate):
    ms = jnp.mean(y * y, axis=-1, keepdims=True)
    return xf + gate * ((y * lax.rsqrt(ms + RMS_EPS)) * post_g)


def _gelu_tanh(x):
    c = math.sqrt(2.0 / math.pi)
    return x * (0.5 * (1.0 + jnp.tanh(c * (x + 0.044715 * (x * x * x)))))


def _ada_body(cs_ref, w_ref, b_ref, o_ref):
    s = cs_ref[...]
    s = s * jax.nn.sigmoid(s)
    m = jnp.dot(s.astype(BF16), w_ref[...].astype(BF16), preferred_element_type=F32) + b_ref[...]
    for row in range(MOD_ROWS):
        o_ref[row] = m[row:row + 1, :]


def _pw1_glu_body(x_ref, gain_ref, shift_ref, scale_ref, w_ref, b_ref, o_ref, *, tm):
    gain, shift, scale = gain_ref[...], shift_ref[...], scale_ref[...]
    rb = tm // 2
    hs = [_modnorm(x_ref[q * rb:(q + 1) * rb], gain, shift, scale).astype(BF16) for q in range(2)]
    for q, h in enumerate(hs):
        a = jnp.dot(h, w_ref[:, :D_MODEL], preferred_element_type=F32) + b_ref[:, :D_MODEL]
        g = jnp.dot(h, w_ref[:, D_MODEL:], preferred_element_type=F32) + b_ref[:, D_MODEL:]
        o_ref[q * rb:(q + 1) * rb] = a * jax.nn.sigmoid(g)


def _conv_mix_body(up_ref, u_ref, un_ref, x_ref, wdw_ref, bdw_ref, lng_ref, lnb_ref, w2_ref, b2_ref,
                   postg_ref, gate_ref, o_ref, ue_scr, cv_scr, *, tm, n_tiles):
    i = pl.program_id(1)
    for s in range(D_MODEL // V7X_LANES):
        lanes = slice(s * V7X_LANES, (s + 1) * V7X_LANES)
        ue_scr[s, 0:HALO_ROWS] = jnp.where(i > 0, up_ref[:, lanes], 0.0)
        ue_scr[s, HALO_ROWS:HALO_ROWS + tm] = u_ref[:, lanes]
        ue_scr[s, HALO_ROWS + tm:] = jnp.where(i < n_tiles - 1, un_ref[:, lanes], 0.0)

    first = HALO_ROWS - CONV_WIDTH // 2

    def conv_step(ci, carry):
        base = pl.multiple_of(ci * CONV_ROWS, CONV_ROWS)
        for s in range(D_MODEL // V7X_LANES):
            lanes = slice(s * V7X_LANES, (s + 1) * V7X_LANES)
            acc = None
            for k in range(CONV_WIDTH):
                shifted = ue_scr[s, pl.ds(base + first + k, CONV_ROWS, stride=1), :]
                term = wdw_ref[k:k + 1, lanes] * shifted
                acc = term if acc is None else acc + term
            cv_scr[pl.ds(base, CONV_ROWS), lanes] = acc + bdw_ref[:, lanes]
        return carry

    lax.fori_loop(0, tm // CONV_ROWS, conv_step, 0)

    u = cv_scr[...]
    uc = u - jnp.mean(u, axis=-1, keepdims=True)
    ln = (uc * lax.rsqrt(jnp.mean(uc * uc, axis=-1, keepdims=True) + LN_EPS)) * lng_ref[...] + lnb_ref[...]
    act = (ln * jax.nn.sigmoid(ln)).astype(BF16)
    y = jnp.dot(act, w2_ref[...], preferred_element_type=F32) + b2_ref[...]
    o_ref[...] = _post_residual(x_ref[...], y, postg_ref[...], gate_ref[...])


def _ffn_body(xp_ref, x_ref, xn_ref, gain_ref, shift_ref, scale_ref, wup_ref, wdw_ref, bdw_ref, wd_ref,
              postg_ref, gate_ref, o_ref, ge_scr, *, tm, n_tiles):
    i = pl.program_id(1)
    gain, shift, scale = gain_ref[...], shift_ref[...], scale_ref[...]
    x = x_ref[...]
    hp = jnp.where(i > 0, _modnorm(xp_ref[...], gain, shift, scale), 0.0).astype(BF16)
    hn = jnp.where(i < n_tiles - 1, _modnorm(xn_ref[...], gain, shift, scale), 0.0).astype(BF16)
    half = tm // 2
    h_lo = _modnorm(x[:half], gain, shift, scale).astype(BF16)
    h_hi = _modnorm(x[half:], gain, shift, scale).astype(BF16)
    h = jnp.concatenate([h_lo, h_hi], axis=0)
    he = jnp.concatenate([hp, h, hn], axis=0)

    n_chunks = D_FF // FFN_CHUNK

    def up_proj(j):
        cols = slice(j * FFN_CHUNK, (j + 1) * FFN_CHUNK)
        vcols = slice(D_FF + j * FFN_CHUNK, D_FF + (j + 1) * FFN_CHUNK)
        if j == 0:
            lo_rows = HALO_ROWS + half
            ge_lo = jnp.dot(he[:lo_rows], wup_ref[:, cols], preferred_element_type=F32)
            val_lo = jnp.dot(h_lo, wup_ref[:, vcols], preferred_element_type=F32)
            ge_hi = jnp.dot(he[lo_rows:], wup_ref[:, cols], preferred_element_type=F32)
            val_hi = jnp.dot(h_hi, wup_ref[:, vcols], preferred_element_type=F32)
            return jnp.concatenate([ge_lo, ge_hi], axis=0), jnp.concatenate([val_lo, val_hi], axis=0)
        ge = jnp.dot(he, wup_ref[:, cols], preferred_element_type=F32)
        val = jnp.dot(h, wup_ref[:, vcols], preferred_element_type=F32)
        return ge, val

    acts = []
    nxt = up_proj(0)
    for j in range(n_chunks):
        ge, val = nxt
        if j + 1 < n_chunks:
            nxt = up_proj(j + 1)
        slot = j % 2
        pieces = []
        for s in range(FFN_CHUNK // V7X_LANES):
            lanes = slice(j * FFN_CHUNK + s * V7X_LANES, j * FFN_CHUNK + (s + 1) * V7X_LANES)
            ge_scr[slot, s] = ge[:, s * V7X_LANES:(s + 1) * V7X_LANES]
            gc = bdw_ref[:, lanes]
            for k in range(FFN_CONV_WIDTH):
                first = HALO_ROWS - FFN_CONV_WIDTH // 2 + k
                gc = gc + wdw_ref[k:k + 1, lanes] * ge_scr[slot, s, pl.ds(first, tm, stride=1), :]
            pieces.append(_gelu_tanh(gc) * val[:, s * V7X_LANES:(s + 1) * V7X_LANES])
        acts.append(jnp.concatenate(pieces, axis=1).astype(BF16))
    act = jnp.concatenate(acts, axis=1)

    rb = tm // FFN_TAIL_BLOCKS
    for q in range(FFN_TAIL_BLOCKS):
        rows = slice(q * rb, (q + 1) * rb)
        y = jnp.dot(act[rows], wd_ref[...], preferred_element_type=F32)
        o_ref[rows] = _post_residual(x[rows], y, postg_ref[...], gate_ref[...])


def _proj_body(x_ref, gain_ref, shift_ref, scale_ref, w_ref, *o_refs, col0):
    h = _modnorm(x_ref[...], gain_ref[...], shift_ref[...], scale_ref[...]).astype(BF16)
    for t, o_ref in enumerate(o_refs):
        cols = slice(col0 + t * D_MODEL, col0 + (t + 1) * D_MODEL)
        o_ref[...] = jnp.dot(h, w_ref[:, cols], preferred_element_type=F32).astype(o_ref.dtype)


def _bias_body(src_ref, o_ref):
    qc = lax.broadcasted_iota(jnp.int32, (GRID_W, ATTN_PAIR), 0)
    kc = lax.broadcasted_iota(jnp.int32, (GRID_W, ATTN_PAIR), 1) & (GRID_W - 1)
    c0 = jnp.clip(qc - NA_KW // 2, 0, GRID_W - NA_KW)
    in_win = (kc >= c0) & (kc < c0 + NA_KW)
    base_shift = V7X_LANES - (NA_KW - 1)
    for s in range(2 * NA_KH - 2):
        for e in range(2):
            src = jnp.broadcast_to(src_ref[s, e], (GRID_W, ATTN_PAIR))
            tile = pltpu.roll(src, base_shift, 1, stride=1, stride_axis=0)
            o_ref[s, e * GRID_W:(e + 1) * GRID_W, :] = jnp.where(in_win, tile, NEG_INF)


def _out_proj_body(a_ref, x_ref, w_ref, postg_ref, gate_ref, o_ref):
    y = jnp.dot(a_ref[...], w_ref[...], preferred_element_type=F32)
    o_ref[...] = _post_residual(x_ref[...], y, postg_ref[...], gate_ref[...])


def _attn_body(q_ref, k_ref, v_ref, kc_ref, vc_ref, bias_ref, o_ref, *, rows, rows_per_step):
    n_loc = NA_KH * GRID_W
    lo = lax.broadcasted_iota(jnp.int32, (GRID_W, ATTN_PAIR), 1) < NA_HEAD_DIM
    nt = (((1,), (1,)), ((), ()))

    def scores(p, rr):
        lanes = slice(p * ATTN_PAIR, (p + 1) * ATTN_PAIR)
        r = pl.program_id(1) * rows_per_step + rr
        r0 = jnp.clip(r - NA_KH // 2, 0, rows - NA_KH)
        row_off = r0 - r + (NA_KH - 1)
        start = pl.multiple_of(r0 * GRID_W, GRID_W)
        qp = q_ref[rr * GRID_W:(rr + 1) * GRID_W, lanes]
        zero = jnp.zeros_like(qp)
        qs = jnp.concatenate([jnp.where(lo, qp, zero), jnp.where(lo, zero, qp)], axis=0)
        bias = jnp.concatenate([bias_ref[p, row_off + 2 * jj] for jj in range(NA_KH // 2)], axis=1)
        k_all = jnp.concatenate([k_ref[pl.ds(start, n_loc), lanes], kc_ref[:, lanes]], axis=0)
        s = lax.dot_general(qs, k_all, nt, preferred_element_type=F32)
        return s[:, :n_loc] + bias, s[:, n_loc:], start

    units = [(p, rr) for p in range(NA_HEADS // 2) for rr in range(rows_per_step)]
    pending = [scores(*units[i]) for i in range(ATTN_LOOKAHEAD)]
    for idx, (p, rr) in enumerate(units):
        s_loc, s_ctx, start = pending.pop(0)
        if idx + ATTN_LOOKAHEAD < len(units):
            pending.append(scores(*units[idx + ATTN_LOOKAHEAD]))
        lanes = slice(p * ATTN_PAIR, (p + 1) * ATTN_PAIR)
        m = jnp.maximum(jnp.max(s_loc, axis=-1, keepdims=True), jnp.max(s_ctx, axis=-1, keepdims=True))
        probs = jnp.concatenate([jnp.exp(s_loc - m).astype(BF16), jnp.exp(s_ctx - m).astype(BF16)], axis=1)
        v_all = jnp.concatenate([v_ref[pl.ds(start, n_loc), lanes], vc_ref[:, lanes]], axis=0)
        v_ext = jnp.concatenate([v_all, jnp.ones_like(v_all)], axis=1)
        o_ext = jnp.dot(probs, v_ext, preferred_element_type=F32)
        o = o_ext[:, :ATTN_PAIR] / o_ext[:, ATTN_PAIR:ATTN_PAIR + 1]
        o_ref[rr * GRID_W:(rr + 1) * GRID_W, lanes] = jnp.where(lo, o[:GRID_W], o[GRID_W:]).astype(o_ref.dtype)


def _tile_rows(t, pref=512):
    return pref if t % pref == 0 else t


def _row_spec(tm, width=D_MODEL):
    return pl.BlockSpec((None, tm, width), lambda b, i: (b, i, 0))


def _halo_specs(tm, t):
    per = tm // HALO_ROWS
    last = t // HALO_ROWS - 1
    prev = pl.BlockSpec((None, HALO_ROWS, D_MODEL), lambda b, i: (b, jnp.maximum(i * per - 1, 0), 0))
    nxt = pl.BlockSpec((None, HALO_ROWS, D_MODEL), lambda b, i: (b, jnp.minimum((i + 1) * per, last), 0))
    return prev, nxt


def _const_spec(shape):
    nd = len(shape)
    return pl.BlockSpec(shape, lambda b, i: (0,) * nd, pipeline_mode=pl.Buffered(1))


def _vec_spec():
    return pl.BlockSpec((1, D_MODEL), lambda b, i: (0, 0))


def _mod_spec(layer, comp, ctx):
    if ctx is None:
        return pl.BlockSpec((None, None, None, 1, D_MODEL), lambda b, i: (layer, comp, b, 0, 0))
    return pl.BlockSpec((None, None, None, 1, D_MODEL), lambda b, i: (layer, comp, ctx, 0, 0))


def _params(nbytes):
    return pltpu.CompilerParams(dimension_semantics=("arbitrary", "arbitrary"), vmem_limit_bytes=_vmem_limit(nbytes))


def _ada_mod(cs, mod_w, mod_b):
    depth = mod_w.shape[0]
    blk = _nbytes((D_MODEL, D_MODEL), F32)
    return pl.pallas_call(
        _ada_body,
        grid=(depth, N_MOD),
        in_specs=[
            pl.BlockSpec((MOD_ROWS, D_MODEL), lambda l, k: (0, 0)),
            pl.BlockSpec((None, D_MODEL, D_MODEL), lambda l, k: (l, 0, k)),
            pl.BlockSpec((None, 1, D_MODEL), lambda l, k: (l, 0, k)),
        ],
        out_specs=pl.BlockSpec((None, None, MOD_ROWS, 1, D_MODEL), lambda l, k: (l, k, 0, 0, 0)),
        out_shape=jax.ShapeDtypeStruct((depth, N_MOD, MOD_ROWS, 1, D_MODEL), F32),
        compiler_params=pltpu.CompilerParams(dimension_semantics=("arbitrary", "arbitrary"),
                                             vmem_limit_bytes=_vmem_limit(4 * blk)),
        name="ada_mod",
    )(cs, mod_w, mod_b.reshape(depth, 1, N_MOD * D_MODEL))


def _pw1_glu(x, gain, mods, layer, ctx, w_bf, b):
    bsz, t, _ = x.shape
    tm = _tile_rows(t, PROJ_ROWS)
    est = 4 * _nbytes((tm, D_MODEL), F32) + _nbytes(w_bf.shape, BF16) + 3 * _nbytes((tm, 2 * D_MODEL), F32)
    return pl.pallas_call(
        functools.partial(_pw1_glu_body, tm=tm),
        grid=(bsz, t // tm),
        in_specs=[_row_spec(tm), _vec_spec(), _mod_spec(layer, 0, ctx), _mod_spec(layer, 1, ctx),
                  _const_spec(w_bf.shape), _const_spec(b.shape)],
        out_specs=_row_spec(tm),
        out_shape=jax.ShapeDtypeStruct((bsz, t, D_MODEL), F32),
        compiler_params=_params(est),
        name="pw1_glu",
    )(x, gain, mods, mods, w_bf, b)


def _conv_mix(u, x, w_dw, b_dw, ln_g, ln_b, w2_bf, b2, post_g, mods, layer, ctx):
    bsz, t, _ = x.shape
    tm = _tile_rows(t)
    n_tiles = t // tm
    prev, nxt = _halo_specs(tm, t)
    est = (8 * _nbytes((tm, D_MODEL), F32) + _nbytes(w2_bf.shape, BF16) + _nbytes(w_dw.shape, F32)
           + 4 * _nbytes((tm, D_MODEL), F32))
    return pl.pallas_call(
        functools.partial(_conv_mix_body, tm=tm, n_tiles=n_tiles),
        grid=(bsz, n_tiles),
        in_specs=[prev, _row_spec(tm), nxt, _row_spec(tm),
                  _const_spec(w_dw.shape), _const_spec(b_dw.shape), _vec_spec(), _vec_spec(),
                  _const_spec(w2_bf.shape), _vec_spec(), _vec_spec(), _mod_spec(layer, 2, ctx)],
        out_specs=_row_spec(tm),
        out_shape=jax.ShapeDtypeStruct((bsz, t, D_MODEL), F32),
        scratch_shapes=[pltpu.VMEM((D_MODEL // V7X_LANES, tm + 2 * HALO_ROWS, V7X_LANES), F32),
                        pltpu.VMEM((tm, D_MODEL), F32)],
        compiler_params=_params(est),
        name="conv_mix",
    )(u, u, u, x, w_dw, b_dw, ln_g, ln_b, w2_bf, b2, post_g, mods)


def _layer_spec(stacked_shape, layer):
    nd = len(stacked_shape) - 1
    return pl.BlockSpec((None,) + tuple(stacked_shape[1:]), lambda b, i: (layer,) + (0,) * nd,
                        pipeline_mode=pl.Buffered(1))


def _ffn(x, gain, post_g, mods, layer, ctx, wup, w_dw, b_dw, wd):
    bsz, t, _ = x.shape
    tm = _tile_rows(t, FFN_ROWS)
    n_tiles = t // tm
    prev, nxt = _halo_specs(tm, t)
    est = (8 * _nbytes((tm, D_MODEL), F32) + _nbytes(wup.shape[1:], BF16) + _nbytes(wd.shape[1:], BF16)
           + _nbytes((tm, D_FF), BF16) + 8 * _nbytes((tm + 2 * HALO_ROWS, FFN_CHUNK), F32))
    return pl.pallas_call(
        functools.partial(_ffn_body, tm=tm, n_tiles=n_tiles),
        grid=(bsz, n_tiles),
        in_specs=[prev, _row_spec(tm), nxt, _vec_spec(), _mod_spec(layer, 3, ctx), _mod_spec(layer, 4, ctx),
                  _layer_spec(wup.shape, layer), _const_spec(w_dw.shape), _const_spec(b_dw.shape),
                  _layer_spec(wd.shape, layer), _vec_spec(), _mod_spec(layer, 5, ctx)],
        out_specs=_row_spec(tm),
        out_shape=jax.ShapeDtypeStruct((bsz, t, D_MODEL), F32),
        scratch_shapes=[pltpu.VMEM((2, FFN_CHUNK // V7X_LANES, tm + 2 * HALO_ROWS, V7X_LANES), F32)],
        compiler_params=_params(est),
        name="conv_ffn",
    )(x, x, x, gain, mods, mods, wup, w_dw, b_dw, wd, post_g, mods)


def _proj(x, gain, mods, layer, ctx, w_bf, col0, n_out):
    bsz, t, _ = x.shape
    tm = _tile_rows(t, PROJ_ROWS)
    est = 2 * _nbytes((tm, D_MODEL), F32) + _nbytes(w_bf.shape, BF16) + (2 * n_out + 4) * _nbytes((tm, D_MODEL), F32)
    return pl.pallas_call(
        functools.partial(_proj_body, col0=col0),
        grid=(bsz, t // tm),
        in_specs=[_row_spec(tm), _vec_spec(), _mod_spec(layer, 0, ctx), _mod_spec(layer, 1, ctx),
                  _const_spec(w_bf.shape)],
        out_specs=[_row_spec(tm)] * n_out,
        out_shape=[jax.ShapeDtypeStruct((bsz, t, D_MODEL), BF16)] * n_out,
        compiler_params=_params(est),
        name="norm_proj",
    )(x, gain, mods, mods, w_bf)


def _out_proj(a, x, w_bf, post_g, mods, layer):
    bsz, t, _ = x.shape
    tm = _tile_rows(t, PROJ_ROWS)
    est = 6 * _nbytes((tm, D_MODEL), F32) + _nbytes(w_bf.shape, BF16)
    return pl.pallas_call(
        _out_proj_body,
        grid=(bsz, t // tm),
        in_specs=[_row_spec(tm), _row_spec(tm), _const_spec(w_bf.shape), _vec_spec(), _mod_spec(layer, 2, None)],
        out_specs=_row_spec(tm),
        out_shape=jax.ShapeDtypeStruct((bsz, t, D_MODEL), F32),
        compiler_params=_params(est),
        name="out_proj",
    )(a, x, w_bf, post_g, mods)


def _attention(q, k, v, kc, vc, bias):
    bsz, t, _ = q.shape
    rows = t // GRID_W
    c_len = kc.shape[1]
    rps = ATTN_ROWS_PER_STEP
    assert rows % rps == 0
    tq = rps * GRID_W
    est = (4 * _nbytes((t, D_MODEL), BF16) + 4 * _nbytes((c_len, D_MODEL), BF16)
           + _nbytes(bias.shape, F32) + 4 * _nbytes((tq, D_MODEL), BF16))
    return pl.pallas_call(
        functools.partial(_attn_body, rows=rows, rows_per_step=rps),
        grid=(bsz, rows // rps),
        in_specs=[
            pl.BlockSpec((None, tq, D_MODEL), lambda b, r: (b, r, 0)),
            pl.BlockSpec((None, t, D_MODEL), lambda b, r: (b, 0, 0)),
            pl.BlockSpec((None, t, D_MODEL), lambda b, r: (b, 0, 0)),
            pl.BlockSpec((None, c_len, D_MODEL), lambda b, r: (b, 0, 0)),
            pl.BlockSpec((None, c_len, D_MODEL), lambda b, r: (b, 0, 0)),
            _const_spec(bias.shape),
        ],
        out_specs=pl.BlockSpec((None, tq, D_MODEL), lambda b, r: (b, r, 0)),
        out_shape=jax.ShapeDtypeStruct((bsz, t, D_MODEL), BF16),
        compiler_params=_params(est),
        name="na_attention",
    )(q, k, v, kc, vc, bias)


def _attn_bias_table(rpb):
    n_pairs = NA_HEADS // 2
    n_s = 2 * NA_KH - 2
    rp = jnp.pad(rpb.astype(F32), ((0, 0), (0, 0), (0, GRID_W - (2 * NA_KW - 1))))
    src = jnp.concatenate([rp[:, :n_s], rp[:, 1:]], axis=-1)
    src = jnp.transpose(src.reshape(n_pairs, 2, n_s, ATTN_PAIR), (0, 2, 1, 3))[:, :, :, None, :]
    return pl.pallas_call(
        _bias_body,
        grid=(n_pairs,),
        in_specs=[pl.BlockSpec((None, n_s, 2, 1, ATTN_PAIR), lambda p: (p, 0, 0, 0, 0))],
        out_specs=pl.BlockSpec((None, n_s, 2 * GRID_W, ATTN_PAIR), lambda p: (p, 0, 0, 0)),
        out_shape=jax.ShapeDtypeStruct((n_pairs, n_s, 2 * GRID_W, ATTN_PAIR), F32),
        name="na_bias",
    )(src)


def kernel(x, c, ctx, c_ctx, mod_w, mod_b, mix_pre_g, mix_post_g, ffn_pre_g, ffn_post_g, cv_w_pw1, cv_b_pw1, cv_w_dw, cv_b_dw, cv_ln_g, cv_ln_b, cv_w_pw2, cv_b_pw2, na_w_qkv, na_w_o, na_rpb, ffn_w_up, ffn_w_dw, ffn_b_dw, ffn_w_down):
    bsz = x.shape[0]
    assert bsz + 1 <= MOD_ROWS and x.shape[2] == D_MODEL and mod_w.shape[0] == DEPTH
    ctx_row = bsz
    cs = jnp.concatenate([c, c_ctx[None, :], jnp.zeros((MOD_ROWS - bsz - 1, D_MODEL), F32)], axis=0)
    mods = _ada_mod(cs, mod_w, mod_b)
    wup_all = ffn_w_up.astype(BF16)
    wd_all = ffn_w_down.astype(BF16)

    h_ctx = ctx
    for i in range(DEPTH):
        last = i == DEPTH - 1
        j = i // 2
        use_na = (i % 2) == 1
        pre_g = mix_pre_g[i][None, :]
        post_g = mix_post_g[i][None, :]
        ffn = (wup_all, ffn_w_dw[i], ffn_b_dw[i][None, :], wd_all)
        if use_na:
            scale = NA_HEAD_DIM ** -0.5
            col_scale = jnp.concatenate([jnp.full((D_MODEL,), scale, F32), jnp.ones((2 * D_MODEL,), F32)])
            w_qkv = (na_w_qkv[j] * col_scale[None, :]).astype(BF16)
            q, k, v = _proj(x, pre_g, mods, i, None, w_qkv, 0, 3)
            kc, vc = _proj(h_ctx, pre_g, mods, i, ctx_row, w_qkv, D_MODEL, 2)
            attn = _attention(q, k, v, kc, vc, _attn_bias_table(na_rpb[j]))
            x = _out_proj(attn, x, na_w_o[j].astype(BF16), post_g, mods, i)
            assert last, "context queries (an attention layer that is not the last) are not implemented"
        else:
            w1 = cv_w_pw1[j].astype(BF16)
            b1 = cv_b_pw1[j][None, :]
            w2 = cv_w_pw2[j].astype(BF16)
            cv = (cv_w_dw[j], cv_b_dw[j][None, :], cv_ln_g[j][None, :], cv_ln_b[j][None, :], w2, cv_b_pw2[j][None, :], post_g)
            u = _pw1_glu(x, pre_g, mods, i, None, w1, b1)
            x = _conv_mix(u, x, *cv, mods, i, None)
            if not last:
                u_ctx = _pw1_glu(h_ctx, pre_g, mods, i, ctx_row, w1, b1)
                h_ctx = _conv_mix(u_ctx, h_ctx, *cv, mods, i, ctx_row)
        x = _ffn(x, ffn_pre_g[i][None, :], ffn_post_g[i][None, :], mods, i, None, *ffn)
        if not last:
            h_ctx = _ffn(h_ctx, ffn_pre_g[i][None, :], ffn_post_g[i][None, :], mods, i, ctx_row, *ffn)
    return x
```

```python
import functools
import math

import jax
import jax.numpy as jnp
from jax import lax
from jax.experimental import pallas as pl
from jax.experimental.pallas import tpu as pltpu

D_MODEL = 1024
DEPTH = 2
GRID_W = 64
CONV_WIDTH = 31
NA_HEADS = 16
NA_HEAD_DIM = D_MODEL // NA_HEADS
NA_KH = 8
NA_KW = 16
D_FF = 2816
FFN_CONV_WIDTH = 3
N_MOD = 6
RMS_EPS = 1e-6
LN_EPS = 1e-5
NEG_INF = -1e30

V7X_LANES = 128
V7X_SUBLANES = 8
V7X_VMEM_BYTES = 64 * 1024 * 1024

HALO_ROWS = 2 * V7X_SUBLANES
MOD_ROWS = 16
FFN_CHUNK = 256
FFN_ROWS = 512
FFN_TAIL_BLOCKS = 2
CONV_ROWS = 128
PROJ_ROWS = 1024
ATTN_PAIR = 2 * NA_HEAD_DIM
ATTN_ROWS_PER_STEP = 4
ATTN_LOOKAHEAD = 3

F32 = jnp.float32
BF16 = jnp.bfloat16


def _vmem_limit(nbytes):
    return int(min(V7X_VMEM_BYTES - 6 * 1024 * 1024, max(2 * nbytes, 24 * 1024 * 1024)))


def _nbytes(shape, dtype):
    return math.prod(shape) * jnp.dtype(dtype).itemsize


def _modnorm(xf, gain, shift, scale):
    ms = jnp.mean(xf * xf, axis=-1, keepdims=True)
    return (xf * lax.rsqrt(ms + RMS_EPS)) * (gain * (1.0 + scale)) + shift


def _post_residual(xf, y, post_g, gate):
    ms = jnp.mean(y * y, axis=-1, keepdims=True)
    return xf + gate * ((y * lax.rsqrt(ms + RMS_EPS)) * post_g)


def _gelu_tanh(x):
    c = math.sqrt(2.0 / math.pi)
    return x * (0.5 * (1.0 + jnp.tanh(c * (x + 0.044715 * (x * x * x)))))


def _dwconv_rows(xe3, taps, pad, n, sub):
    out = None
    for r in range(V7X_SUBLANES):
        offs = [d for d in range(-pad, pad + 1) if d % V7X_SUBLANES == r]
        if not offs:
            continue
        rr = xe3 if r == 0 else pltpu.roll(xe3, V7X_SUBLANES - r, axis=1)
        m = n if r == 0 else n + 1
        part = None
        for d in offs:
            a = d // V7X_SUBLANES
            term = taps[d + pad] * rr[2 + a: 2 + a + m]
            part = term if part is None else part + term
        if r != 0:
            part = jnp.where(sub < V7X_SUBLANES - r, part[0:n], part[1:n + 1])
        out = part if out is None else out + part
    return out


def _ada_body(cs_ref, w_ref, b_ref, o_ref):
    s = cs_ref[...]
    s = s * jax.nn.sigmoid(s)
    m = jnp.dot(s.astype(BF16), w_ref[...].astype(BF16), preferred_element_type=F32) + b_ref[...]
    for row in range(MOD_ROWS):
        o_ref[row] = m[row:row + 1, :]


def _pw1_glu_body(x_ref, gain_ref, shift_ref, scale_ref, w_ref, b_ref, o_ref, *, tm):
    gain, shift, scale = gain_ref[...], shift_ref[...], scale_ref[...]
    rb = tm // 2
    hs = [_modnorm(x_ref[q * rb:(q + 1) * rb], gain, shift, scale).astype(BF16) for q in range(2)]
    for q, h in enumerate(hs):
        a = jnp.dot(h, w_ref[:, :D_MODEL], preferred_element_type=F32) + b_ref[:, :D_MODEL]
        g = jnp.dot(h, w_ref[:, D_MODEL:], preferred_element_type=F32) + b_ref[:, D_MODEL:]
        o_ref[q * rb:(q + 1) * rb] = a * jax.nn.sigmoid(g)


def _conv_mix_body(up_ref, u_ref, un_ref, x_ref, wdw_ref, bdw_ref, lng_ref, lnb_ref, w2_ref, b2_ref,
                   postg_ref, gate_ref, o_ref, ue_scr, cv_scr, *, tm, n_tiles):
    i = pl.program_id(1)
    for s in range(D_MODEL // V7X_LANES):
        lanes = slice(s * V7X_LANES, (s + 1) * V7X_LANES)
        ue_scr[s, 0:HALO_ROWS] = jnp.where(i > 0, up_ref[:, lanes], 0.0)
        ue_scr[s, HALO_ROWS:HALO_ROWS + tm] = u_ref[:, lanes]
        ue_scr[s, HALO_ROWS + tm:] = jnp.where(i < n_tiles - 1, un_ref[:, lanes], 0.0)

    first = HALO_ROWS - CONV_WIDTH // 2

    def conv_step(ci, carry):
        base = pl.multiple_of(ci * CONV_ROWS, CONV_ROWS)
        for s in range(D_MODEL // V7X_LANES):
            lanes = slice(s * V7X_LANES, (s + 1) * V7X_LANES)
            acc = None
            for k in range(CONV_WIDTH):
                shifted = ue_scr[s, pl.ds(base + first + k, CONV_ROWS, stride=1), :]
                term = wdw_ref[k, 0:1, lanes] * shifted
                acc = term if acc is None else acc + term
            cv_scr[pl.ds(base, CONV_ROWS), lanes] = acc + bdw_ref[0:1, lanes]
        return carry

    lax.fori_loop(0, tm // CONV_ROWS, conv_step, 0)

    u = cv_scr[...]
    uc = u - jnp.mean(u, axis=-1, keepdims=True)
    ln = (uc * lax.rsqrt(jnp.mean(uc * uc, axis=-1, keepdims=True) + LN_EPS)) * lng_ref[...] + lnb_ref[...]
    act = (ln * jax.nn.sigmoid(ln)).astype(BF16)
    y = jnp.dot(act, w2_ref[...], preferred_element_type=F32) + b2_ref[...]
    o_ref[...] = _post_residual(x_ref[...], y, postg_ref[...], gate_ref[...])


def _ffn_body(xp_ref, x_ref, xn_ref, gain_ref, shift_ref, scale_ref, wup_ref, wdw_ref, bdw_ref, wd_ref,
              postg_ref, gate_ref, o_ref, *, tm, n_tiles):
    i = pl.program_id(1)
    gain, shift, scale = gain_ref[...], shift_ref[...], scale_ref[...]
    x = x_ref[...]
    hp = jnp.where(i > 0, _modnorm(xp_ref[...], gain, shift, scale), 0.0).astype(BF16)
    hn = jnp.where(i < n_tiles - 1, _modnorm(xn_ref[...], gain, shift, scale), 0.0).astype(BF16)
    half = tm // 2
    h_lo = _modnorm(x[:half], gain, shift, scale).astype(BF16)
    h_hi = _modnorm(x[half:], gain, shift, scale).astype(BF16)
    h = jnp.concatenate([h_lo, h_hi], axis=0)
    he = jnp.concatenate([hp, h, hn], axis=0)

    n = tm // V7X_SUBLANES
    sub = lax.broadcasted_iota(jnp.int32, (n, V7X_SUBLANES, FFN_CHUNK), 1)
    n_chunks = D_FF // FFN_CHUNK

    def up_proj(j):
        cols = slice(j * FFN_CHUNK, (j + 1) * FFN_CHUNK)
        vcols = slice(D_FF + j * FFN_CHUNK, D_FF + (j + 1) * FFN_CHUNK)
        if j == 0:
            lo_rows = HALO_ROWS + half
            ge_lo = jnp.dot(he[:lo_rows], wup_ref[:, cols], preferred_element_type=F32)
            val_lo = jnp.dot(h_lo, wup_ref[:, vcols], preferred_element_type=F32)
            ge_hi = jnp.dot(he[lo_rows:], wup_ref[:, cols], preferred_element_type=F32)
            val_hi = jnp.dot(h_hi, wup_ref[:, vcols], preferred_element_type=F32)
            return jnp.concatenate([ge_lo, ge_hi], axis=0), jnp.concatenate([val_lo, val_hi], axis=0)
        ge = jnp.dot(he, wup_ref[:, cols], preferred_element_type=F32)
        val = jnp.dot(h, wup_ref[:, vcols], preferred_element_type=F32)
        return ge, val

    acts = []
    nxt = up_proj(0)
    for j in range(n_chunks):
        ge, val = nxt
        if j + 1 < n_chunks:
            nxt = up_proj(j + 1)
        cols = slice(j * FFN_CHUNK, (j + 1) * FFN_CHUNK)
        ge3 = ge.reshape(n + 4, V7X_SUBLANES, FFN_CHUNK)
        taps = [wdw_ref[k, :, cols] for k in range(FFN_CONV_WIDTH)]
        gc = _dwconv_rows(ge3, taps, FFN_CONV_WIDTH // 2, n, sub) + bdw_ref[:, cols]
        acts.append((_gelu_tanh(gc) * val.reshape(n, V7X_SUBLANES, FFN_CHUNK)).reshape(tm, FFN_CHUNK).astype(BF16))
    act = jnp.concatenate(acts, axis=1)

    rb = tm // FFN_TAIL_BLOCKS
    for q in range(FFN_TAIL_BLOCKS):
        rows = slice(q * rb, (q + 1) * rb)
        y = jnp.dot(act[rows], wd_ref[...], preferred_element_type=F32)
        o_ref[rows] = _post_residual(x[rows], y, postg_ref[...], gate_ref[...])


def _proj_body(x_ref, gain_ref, shift_ref, scale_ref, w_ref, *o_refs, col0):
    h = _modnorm(x_ref[...], gain_ref[...], shift_ref[...], scale_ref[...]).astype(BF16)
    for t, o_ref in enumerate(o_refs):
        cols = slice(col0 + t * D_MODEL, col0 + (t + 1) * D_MODEL)
        o_ref[...] = jnp.dot(h, w_ref[:, cols], preferred_element_type=F32).astype(o_ref.dtype)


def _bias_body(src_ref, o_ref):
    qc = lax.broadcasted_iota(jnp.int32, (GRID_W, ATTN_PAIR), 0)
    kc = lax.broadcasted_iota(jnp.int32, (GRID_W, ATTN_PAIR), 1) & (GRID_W - 1)
    c0 = jnp.clip(qc - NA_KW // 2, 0, GRID_W - NA_KW)
    in_win = (kc >= c0) & (kc < c0 + NA_KW)
    base_shift = V7X_LANES - (NA_KW - 1)
    for s in range(2 * NA_KH - 2):
        for e in range(2):
            src = jnp.broadcast_to(src_ref[s, e], (GRID_W, ATTN_PAIR))
            tile = pltpu.roll(src, base_shift, 1, stride=1, stride_axis=0)
            o_ref[s, e * GRID_W:(e + 1) * GRID_W, :] = jnp.where(in_win, tile, NEG_INF)


def _out_proj_body(a_ref, x_ref, w_ref, postg_ref, gate_ref, o_ref):
    y = jnp.dot(a_ref[...], w_ref[...], preferred_element_type=F32)
    o_ref[...] = _post_residual(x_ref[...], y, postg_ref[...], gate_ref[...])


def _attn_body(q_ref, k_ref, v_ref, kc_ref, vc_ref, bias_ref, o_ref, *, rows, rows_per_step):
    n_loc = NA_KH * GRID_W
    lo = lax.broadcasted_iota(jnp.int32, (GRID_W, ATTN_PAIR), 1) < NA_HEAD_DIM
    nt = (((1,), (1,)), ((), ()))

    def scores(p, rr):
        lanes = slice(p * ATTN_PAIR, (p + 1) * ATTN_PAIR)
        r = pl.program_id(1) * rows_per_step + rr
        r0 = jnp.clip(r - NA_KH // 2, 0, rows - NA_KH)
        row_off = r0 - r + (NA_KH - 1)
        start = pl.multiple_of(r0 * GRID_W, GRID_W)
        qp = q_ref[rr * GRID_W:(rr + 1) * GRID_W, lanes]
        zero = jnp.zeros_like(qp)
        qs = jnp.concatenate([jnp.where(lo, qp, zero), jnp.where(lo, zero, qp)], axis=0)
        bias = jnp.concatenate([bias_ref[p, row_off + 2 * jj] for jj in range(NA_KH // 2)], axis=1)
        k_all = jnp.concatenate([k_ref[pl.ds(start, n_loc), lanes], kc_ref[:, lanes]], axis=0)
        s = lax.dot_general(qs, k_all, nt, preferred_element_type=F32)
        return s[:, :n_loc] + bias, s[:, n_loc:], start

    units = [(p, rr) for p in range(NA_HEADS // 2) for rr in range(rows_per_step)]
    pending = [scores(*units[i]) for i in range(ATTN_LOOKAHEAD)]
    for idx, (p, rr) in enumerate(units):
        s_loc, s_ctx, start = pending.pop(0)
        if idx + ATTN_LOOKAHEAD < len(units):
            pending.append(scores(*units[idx + ATTN_LOOKAHEAD]))
        lanes = slice(p * ATTN_PAIR, (p + 1) * ATTN_PAIR)
        m = jnp.maximum(jnp.max(s_loc, axis=-1, keepdims=True), jnp.max(s_ctx, axis=-1, keepdims=True))
        probs = jnp.concatenate([jnp.exp(s_loc - m).astype(BF16), jnp.exp(s_ctx - m).astype(BF16)], axis=1)
        v_all = jnp.concatenate([v_ref[pl.ds(start, n_loc), lanes], vc_ref[:, lanes]], axis=0)
        v_ext = jnp.concatenate([v_all, jnp.ones_like(v_all)], axis=1)
        o_ext = jnp.dot(probs, v_ext, preferred_element_type=F32)
        o = o_ext[:, :ATTN_PAIR] / o_ext[:, ATTN_PAIR:ATTN_PAIR + 1]
        o_ref[rr * GRID_W:(rr + 1) * GRID_W, lanes] = jnp.where(lo, o[:GRID_W], o[GRID_W:]).astype(o_ref.dtype)


def _tile_rows(t, pref=512):
    return pref if t % pref == 0 else t


def _row_spec(tm, width=D_MODEL):
    return pl.BlockSpec((None, tm, width), lambda b, i: (b, i, 0))


def _halo_specs(tm, t):
    per = tm // HALO_ROWS
    last = t // HALO_ROWS - 1
    prev = pl.BlockSpec((None, HALO_ROWS, D_MODEL), lambda b, i: (b, jnp.maximum(i * per - 1, 0), 0))
    nxt = pl.BlockSpec((None, HALO_ROWS, D_MODEL), lambda b, i: (b, jnp.minimum((i + 1) * per, last), 0))
    return prev, nxt


def _const_spec(shape):
    nd = len(shape)
    return pl.BlockSpec(shape, lambda b, i: (0,) * nd, pipeline_mode=pl.Buffered(1))


def _vec_spec():
    return pl.BlockSpec((1, D_MODEL), lambda b, i: (0, 0))


def _mod_spec(layer, comp, ctx):
    if ctx is None:
        return pl.BlockSpec((None, None, None, 1, D_MODEL), lambda b, i: (layer, comp, b, 0, 0))
    return pl.BlockSpec((None, None, None, 1, D_MODEL), lambda b, i: (layer, comp, ctx, 0, 0))


def _params(nbytes):
    return pltpu.CompilerParams(dimension_semantics=("arbitrary", "arbitrary"), vmem_limit_bytes=_vmem_limit(nbytes))


def _ada_mod(cs, mod_w, mod_b):
    depth = mod_w.shape[0]
    blk = _nbytes((D_MODEL, D_MODEL), F32)
    return pl.pallas_call(
        _ada_body,
        grid=(depth, N_MOD),
        in_specs=[
            pl.BlockSpec((MOD_ROWS, D_MODEL), lambda l, k: (0, 0)),
            pl.BlockSpec((None, D_MODEL, D_MODEL), lambda l, k: (l, 0, k)),
            pl.BlockSpec((None, 1, D_MODEL), lambda l, k: (l, 0, k)),
        ],
        out_specs=pl.BlockSpec((None, None, MOD_ROWS, 1, D_MODEL), lambda l, k: (l, k, 0, 0, 0)),
        out_shape=jax.ShapeDtypeStruct((depth, N_MOD, MOD_ROWS, 1, D_MODEL), F32),
        compiler_params=pltpu.CompilerParams(dimension_semantics=("arbitrary", "arbitrary"),
                                             vmem_limit_bytes=_vmem_limit(4 * blk)),
        name="ada_mod",
    )(cs, mod_w, mod_b.reshape(depth, 1, N_MOD * D_MODEL))


def _pw1_glu(x, gain, mods, layer, ctx, w_bf, b):
    bsz, t, _ = x.shape
    tm = _tile_rows(t, PROJ_ROWS)
    est = 4 * _nbytes((tm, D_MODEL), F32) + _nbytes(w_bf.shape, BF16) + 3 * _nbytes((tm, 2 * D_MODEL), F32)
    return pl.pallas_call(
        functools.partial(_pw1_glu_body, tm=tm),
        grid=(bsz, t // tm),
        in_specs=[_row_spec(tm), _vec_spec(), _mod_spec(layer, 0, ctx), _mod_spec(layer, 1, ctx),
                  _const_spec(w_bf.shape), _const_spec(b.shape)],
        out_specs=_row_spec(tm),
        out_shape=jax.ShapeDtypeStruct((bsz, t, D_MODEL), F32),
        compiler_params=_params(est),
        name="pw1_glu",
    )(x, gain, mods, mods, w_bf, b)


def _conv_mix(u, x, w_dw, b_dw, ln_g, ln_b, w2_bf, b2, post_g, mods, layer, ctx):
    bsz, t, _ = x.shape
    tm = _tile_rows(t)
    n_tiles = t // tm
    prev, nxt = _halo_specs(tm, t)
    est = (8 * _nbytes((tm, D_MODEL), F32) + _nbytes(w2_bf.shape, BF16) + _nbytes(w_dw.shape, F32)
           + 4 * _nbytes((tm, D_MODEL), F32))
    return pl.pallas_call(
        functools.partial(_conv_mix_body, tm=tm, n_tiles=n_tiles),
        grid=(bsz, n_tiles),
        in_specs=[prev, _row_spec(tm), nxt, _row_spec(tm),
                  _const_spec(w_dw.shape), _const_spec(b_dw.shape), _vec_spec(), _vec_spec(),
                  _const_spec(w2_bf.shape), _vec_spec(), _vec_spec(), _mod_spec(layer, 2, ctx)],
        out_specs=_row_spec(tm),
        out_shape=jax.ShapeDtypeStruct((bsz, t, D_MODEL), F32),
        scratch_shapes=[pltpu.VMEM((D_MODEL // V7X_LANES, tm + 2 * HALO_ROWS, V7X_LANES), F32),
                        pltpu.VMEM((tm, D_MODEL), F32)],
        compiler_params=_params(est),
        name="conv_mix",
    )(u, u, u, x, w_dw, b_dw, ln_g, ln_b, w2_bf, b2, post_g, mods)


def _layer_spec(stacked_shape, layer):
    nd = len(stacked_shape) - 1
    return pl.BlockSpec((None,) + tuple(stacked_shape[1:]), lambda b, i: (layer,) + (0,) * nd,
                        pipeline_mode=pl.Buffered(1))


def _ffn(x, gain, post_g, mods, layer, ctx, wup, w_dw, b_dw, wd):
    bsz, t, _ = x.shape
    tm = _tile_rows(t, FFN_ROWS)
    n_tiles = t // tm
    prev, nxt = _halo_specs(tm, t)
    est = (8 * _nbytes((tm, D_MODEL), F32) + _nbytes(wup.shape[1:], BF16) + _nbytes(wd.shape[1:], BF16)
           + _nbytes((tm, D_FF), BF16) + 8 * _nbytes((tm + 2 * HALO_ROWS, FFN_CHUNK), F32))
    return pl.pallas_call(
        functools.partial(_ffn_body, tm=tm, n_tiles=n_tiles),
        grid=(bsz, n_tiles),
        in_specs=[prev, _row_spec(tm), nxt, _vec_spec(), _mod_spec(layer, 3, ctx), _mod_spec(layer, 4, ctx),
                  _layer_spec(wup.shape, layer), _const_spec(w_dw.shape), _const_spec(b_dw.shape),
                  _layer_spec(wd.shape, layer), _vec_spec(), _mod_spec(layer, 5, ctx)],
        out_specs=_row_spec(tm),
        out_shape=jax.ShapeDtypeStruct((bsz, t, D_MODEL), F32),
        compiler_params=_params(est),
        name="conv_ffn",
    )(x, x, x, gain, mods, mods, wup, w_dw, b_dw, wd, post_g, mods)


def _proj(x, gain, mods, layer, ctx, w_bf, col0, n_out):
    bsz, t, _ = x.shape
    tm = _tile_rows(t, PROJ_ROWS)
    est = 2 * _nbytes((tm, D_MODEL), F32) + _nbytes(w_bf.shape, BF16) + (2 * n_out + 4) * _nbytes((tm, D_MODEL), F32)
    return pl.pallas_call(
        functools.partial(_proj_body, col0=col0),
        grid=(bsz, t // tm),
        in_specs=[_row_spec(tm), _vec_spec(), _mod_spec(layer, 0, ctx), _mod_spec(layer, 1, ctx),
                  _const_spec(w_bf.shape)],
        out_specs=[_row_spec(tm)] * n_out,
        out_shape=[jax.ShapeDtypeStruct((bsz, t, D_MODEL), BF16)] * n_out,
        compiler_params=_params(est),
        name="norm_proj",
    )(x, gain, mods, mods, w_bf)


def _out_proj(a, x, w_bf, post_g, mods, layer):
    bsz, t, _ = x.shape
    tm = _tile_rows(t, PROJ_ROWS)
    est = 6 * _nbytes((tm, D_MODEL), F32) + _nbytes(w_bf.shape, BF16)
    return pl.pallas_call(
        _out_proj_body,
        grid=(bsz, t // tm),
        in_specs=[_row_spec(tm), _row_spec(tm), _const_spec(w_bf.shape), _vec_spec(), _mod_spec(layer, 2, None)],
        out_specs=_row_spec(tm),
        out_shape=jax.ShapeDtypeStruct((bsz, t, D_MODEL), F32),
        compiler_params=_params(est),
        name="out_proj",
    )(a, x, w_bf, post_g, mods)


def _attention(q, k, v, kc, vc, bias):
    bsz, t, _ = q.shape
    rows = t // GRID_W
    c_len = kc.shape[1]
    rps = ATTN_ROWS_PER_STEP
    assert rows % rps == 0
    tq = rps * GRID_W
    est = (4 * _nbytes((t, D_MODEL), BF16) + 4 * _nbytes((c_len, D_MODEL), BF16)
           + _nbytes(bias.shape, F32) + 4 * _nbytes((tq, D_MODEL), BF16))
    return pl.pallas_call(
        functools.partial(_attn_body, rows=rows, rows_per_step=rps),
        grid=(bsz, rows // rps),
        in_specs=[
            pl.BlockSpec((None, tq, D_MODEL), lambda b, r: (b, r, 0)),
            pl.BlockSpec((None, t, D_MODEL), lambda b, r: (b, 0, 0)),
            pl.BlockSpec((None, t, D_MODEL), lambda b, r: (b, 0, 0)),
            pl.BlockSpec((None, c_len, D_MODEL), lambda b, r: (b, 0, 0)),
            pl.BlockSpec((None, c_len, D_MODEL), lambda b, r: (b, 0, 0)),
            _const_spec(bias.shape),
        ],
        out_specs=pl.BlockSpec((None, tq, D_MODEL), lambda b, r: (b, r, 0)),
        out_shape=jax.ShapeDtypeStruct((bsz, t, D_MODEL), BF16),
        compiler_params=_params(est),
        name="na_attention",
    )(q, k, v, kc, vc, bias)


def _attn_bias_table(rpb):
    n_pairs = NA_HEADS // 2
    n_s = 2 * NA_KH - 2
    rp = jnp.pad(rpb.astype(F32), ((0, 0), (0, 0), (0, GRID_W - (2 * NA_KW - 1))))
    src = jnp.concatenate([rp[:, :n_s], rp[:, 1:]], axis=-1)
    src = jnp.transpose(src.reshape(n_pairs, 2, n_s, ATTN_PAIR), (0, 2, 1, 3))[:, :, :, None, :]
    return pl.pallas_call(
        _bias_body,
        grid=(n_pairs,),
        in_specs=[pl.BlockSpec((None, n_s, 2, 1, ATTN_PAIR), lambda p: (p, 0, 0, 0, 0))],
        out_specs=pl.BlockSpec((None, n_s, 2 * GRID_W, ATTN_PAIR), lambda p: (p, 0, 0, 0)),
        out_shape=jax.ShapeDtypeStruct((n_pairs, n_s, 2 * GRID_W, ATTN_PAIR), F32),
        name="na_bias",
    )(src)


def kernel(x, c, ctx, c_ctx, mod_w, mod_b, mix_pre_g, mix_post_g, ffn_pre_g, ffn_post_g, cv_w_pw1, cv_b_pw1, cv_w_dw, cv_b_dw, cv_ln_g, cv_ln_b, cv_w_pw2, cv_b_pw2, na_w_qkv, na_w_o, na_rpb, ffn_w_up, ffn_w_dw, ffn_b_dw, ffn_w_down):
    bsz = x.shape[0]
    assert bsz + 1 <= MOD_ROWS and x.shape[2] == D_MODEL and mod_w.shape[0] == DEPTH
    ctx_row = bsz
    cs = jnp.concatenate([c, c_ctx[None, :], jnp.zeros((MOD_ROWS - bsz - 1, D_MODEL), F32)], axis=0)
    mods = _ada_mod(cs, mod_w, mod_b)
    wup_all = ffn_w_up.astype(BF16)
    wd_all = ffn_w_down.astype(BF16)

    h_ctx = ctx
    for i in range(DEPTH):
        last = i == DEPTH - 1
        j = i // 2
        use_na = (i % 2) == 1
        pre_g = mix_pre_g[i][None, :]
        post_g = mix_post_g[i][None, :]
        ffn = (wup_all, jnp.broadcast_to(ffn_w_dw[i][:, None, :], (FFN_CONV_WIDTH, V7X_SUBLANES, D_FF)),
               jnp.broadcast_to(ffn_b_dw[i][None, :], (V7X_SUBLANES, D_FF)), wd_all)
        if use_na:
            scale = NA_HEAD_DIM ** -0.5
            col_scale = jnp.concatenate([jnp.full((D_MODEL,), scale, F32), jnp.ones((2 * D_MODEL,), F32)])
            w_qkv = (na_w_qkv[j] * col_scale[None, :]).astype(BF16)
            q, k, v = _proj(x, pre_g, mods, i, None, w_qkv, 0, 3)
            kc, vc = _proj(h_ctx, pre_g, mods, i, ctx_row, w_qkv, D_MODEL, 2)
            attn = _attention(q, k, v, kc, vc, _attn_bias_table(na_rpb[j]))
            x = _out_proj(attn, x, na_w_o[j].astype(BF16), post_g, mods, i)
            assert last, "context queries (an attention layer that is not the last) are not implemented"
        else:
            w1 = cv_w_pw1[j].astype(BF16)
            b1 = cv_b_pw1[j][None, :]
            w_dw = jnp.broadcast_to(cv_w_dw[j][:, None, :], (CONV_WIDTH, V7X_SUBLANES, D_MODEL))
            b_dw = jnp.broadcast_to(cv_b_dw[j][None, :], (V7X_SUBLANES, D_MODEL))
            w2 = cv_w_pw2[j].astype(BF16)
            cv = (w_dw, b_dw, cv_ln_g[j][None, :], cv_ln_b[j][None, :], w2, cv_b_pw2[j][None, :], post_g)
            u = _pw1_glu(x, pre_g, mods, i, None, w1, b1)
            x = _conv_mix(u, x, *cv, mods, i, None)
            if not last:
                u_ctx = _pw1_glu(h_ctx, pre_g, mods, i, ctx_row, w1, b1)
                h_ctx = _conv_mix(u_ctx, h_ctx, *cv, mods, i, ctx_row)
        x = _ffn(x, ffn_pre_g[i][None, :], ffn_post_g[i][None, :], mods, i, None, *ffn)
        if not last:
            h_ctx = _ffn(h_ctx, ffn_pre_g[i][None, :], ffn_post_g[i][None, :], mods, i, ctx_row, *ffn)
    return x
```

```python
import functools
import math

import jax
import jax.numpy as jnp
from jax import lax
from jax.experimental import pallas as pl
from jax.experimental.pallas import tpu as pltpu

D_MODEL = 1024
DEPTH = 2
GRID_W = 64
CONV_WIDTH = 31
NA_HEADS = 16
NA_HEAD_DIM = D_MODEL // NA_HEADS
NA_KH = 8
NA_KW = 16
D_FF = 2816
FFN_CONV_WIDTH = 3
N_MOD = 6
RMS_EPS = 1e-6
LN_EPS = 1e-5
NEG_INF = -1e30

V7X_LANES = 128
V7X_SUBLANES = 8
V7X_VMEM_BYTES = 64 * 1024 * 1024

HALO_ROWS = 2 * V7X_SUBLANES
MOD_ROWS = 16
FFN_CHUNK = 256
FFN_ROWS = 1024
FFN_TAIL_BLOCKS = 2
CONV_ROWS = 128
PROJ_ROWS = 1024
ATTN_PAIR = 2 * NA_HEAD_DIM
ATTN_ROWS_PER_STEP = 4
ATTN_LOOKAHEAD = 3

F32 = jnp.float32
BF16 = jnp.bfloat16


def _vmem_limit(nbytes):
    return int(min(V7X_VMEM_BYTES - 6 * 1024 * 1024, max(2 * nbytes, 24 * 1024 * 1024)))


def _nbytes(shape, dtype):
    return math.prod(shape) * jnp.dtype(dtype).itemsize


def _modnorm(xf, gain, shift, scale):
    ms = jnp.mean(xf * xf, axis=-1, keepdims=True)
    return (xf * lax.rsqrt(ms + RMS_EPS)) * (gain * (1.0 + scale)) + shift


def _post_residual(xf, y, post_g, gate):
    ms = jnp.mean(y * y, axis=-1, keepdims=True)
    return xf + gate * ((y * lax.rsqrt(ms + RMS_EPS)) * post_g)


def _gelu_tanh(x):
    c = math.sqrt(2.0 / math.pi)
    return x * (0.5 * (1.0 + jnp.tanh(c * (x + 0.044715 * (x * x * x)))))


def _dwconv_rows(xe3, taps, pad, n, sub):
    out = None
    for r in range(V7X_SUBLANES):
        offs = [d for d in range(-pad, pad + 1) if d % V7X_SUBLANES == r]
        if not offs:
            continue
        rr = xe3 if r == 0 else pltpu.roll(xe3, V7X_SUBLANES - r, axis=1)
        m = n if r == 0 else n + 1
        part = None
        for d in offs:
            a = d // V7X_SUBLANES
            term = taps[d + pad] * rr[2 + a: 2 + a + m]
            part = term if part is None else part + term
        if r != 0:
            part = jnp.where(sub < V7X_SUBLANES - r, part[0:n], part[1:n + 1])
        out = part if out is None else out + part
    return out


def _ada_body(cs_ref, w_ref, b_ref, o_ref):
    s = cs_ref[...]
    s = s * jax.nn.sigmoid(s)
    m = jnp.dot(s.astype(BF16), w_ref[...].astype(BF16), preferred_element_type=F32) + b_ref[...]
    for row in range(MOD_ROWS):
        o_ref[row] = m[row:row + 1, :]


def _pw1_glu_body(x_ref, gain_ref, shift_ref, scale_ref, w_ref, b_ref, o_ref, *, tm):
    gain, shift, scale = gain_ref[...], shift_ref[...], scale_ref[...]
    rb = tm // 2
    hs = [_modnorm(x_ref[q * rb:(q + 1) * rb], gain, shift, scale).astype(BF16) for q in range(2)]
    for q, h in enumerate(hs):
        a = jnp.dot(h, w_ref[:, :D_MODEL], preferred_element_type=F32) + b_ref[:, :D_MODEL]
        g = jnp.dot(h, w_ref[:, D_MODEL:], preferred_element_type=F32) + b_ref[:, D_MODEL:]
        o_ref[q * rb:(q + 1) * rb] = a * jax.nn.sigmoid(g)


def _conv_mix_body(up_ref, u_ref, un_ref, x_ref, wdw_ref, bdw_ref, lng_ref, lnb_ref, w2_ref, b2_ref,
                   postg_ref, gate_ref, o_ref, ue_scr, cv_scr, *, tm, n_tiles):
    i = pl.program_id(1)
    for s in range(D_MODEL // V7X_LANES):
        lanes = slice(s * V7X_LANES, (s + 1) * V7X_LANES)
        ue_scr[s, 0:HALO_ROWS] = jnp.where(i > 0, up_ref[:, lanes], 0.0)
        ue_scr[s, HALO_ROWS:HALO_ROWS + tm] = u_ref[:, lanes]
        ue_scr[s, HALO_ROWS + tm:] = jnp.where(i < n_tiles - 1, un_ref[:, lanes], 0.0)

    first = HALO_ROWS - CONV_WIDTH // 2

    def conv_step(ci, carry):
        base = pl.multiple_of(ci * CONV_ROWS, CONV_ROWS)
        for s in range(D_MODEL // V7X_LANES):
            lanes = slice(s * V7X_LANES, (s + 1) * V7X_LANES)
            acc = None
            for k in range(CONV_WIDTH):
                shifted = ue_scr[s, pl.ds(base + first + k, CONV_ROWS, stride=1), :]
                term = wdw_ref[k, 0:1, lanes] * shifted
                acc = term if acc is None else acc + term
            cv_scr[pl.ds(base, CONV_ROWS), lanes] = acc + bdw_ref[0:1, lanes]
        return carry

    lax.fori_loop(0, tm // CONV_ROWS, conv_step, 0)

    u = cv_scr[...]
    uc = u - jnp.mean(u, axis=-1, keepdims=True)
    ln = (uc * lax.rsqrt(jnp.mean(uc * uc, axis=-1, keepdims=True) + LN_EPS)) * lng_ref[...] + lnb_ref[...]
    act = (ln * jax.nn.sigmoid(ln)).astype(BF16)
    y = jnp.dot(act, w2_ref[...], preferred_element_type=F32) + b2_ref[...]
    o_ref[...] = _post_residual(x_ref[...], y, postg_ref[...], gate_ref[...])


def _ffn_body(xp_ref, x_ref, xn_ref, gain_ref, shift_ref, scale_ref, wup_ref, wdw_ref, bdw_ref, wd_ref,
              postg_ref, gate_ref, o_ref, *, tm, n_tiles):
    i = pl.program_id(1)
    gain, shift, scale = gain_ref[...], shift_ref[...], scale_ref[...]
    x = x_ref[...]
    hp = jnp.where(i > 0, _modnorm(xp_ref[...], gain, shift, scale), 0.0).astype(BF16)
    hn = jnp.where(i < n_tiles - 1, _modnorm(xn_ref[...], gain, shift, scale), 0.0).astype(BF16)
    half = tm // 2
    h_lo = _modnorm(x[:half], gain, shift, scale).astype(BF16)
    h_hi = _modnorm(x[half:], gain, shift, scale).astype(BF16)
    h = jnp.concatenate([h_lo, h_hi], axis=0)
    he = jnp.concatenate([hp, h, hn], axis=0)

    n = tm // V7X_SUBLANES
    sub = lax.broadcasted_iota(jnp.int32, (n, V7X_SUBLANES, FFN_CHUNK), 1)
    n_chunks = D_FF // FFN_CHUNK

    def up_proj(j):
        cols = slice(j * FFN_CHUNK, (j + 1) * FFN_CHUNK)
        vcols = slice(D_FF + j * FFN_CHUNK, D_FF + (j + 1) * FFN_CHUNK)
        if j == 0:
            lo_rows = HALO_ROWS + half
            ge_lo = jnp.dot(he[:lo_rows], wup_ref[:, cols], preferred_element_type=F32)
            val_lo = jnp.dot(h_lo, wup_ref[:, vcols], preferred_element_type=F32)
            ge_hi = jnp.dot(he[lo_rows:], wup_ref[:, cols], preferred_element_type=F32)
            val_hi = jnp.dot(h_hi, wup_ref[:, vcols], preferred_element_type=F32)
            return jnp.concatenate([ge_lo, ge_hi], axis=0), jnp.concatenate([val_lo, val_hi], axis=0)
        ge = jnp.dot(he, wup_ref[:, cols], preferred_element_type=F32)
        val = jnp.dot(h, wup_ref[:, vcols], preferred_element_type=F32)
        return ge, val

    acts = []
    nxt = up_proj(0)
    for j in range(n_chunks):
        ge, val = nxt
        if j + 1 < n_chunks:
            nxt = up_proj(j + 1)
        cols = slice(j * FFN_CHUNK, (j + 1) * FFN_CHUNK)
        ge3 = ge.reshape(n + 4, V7X_SUBLANES, FFN_CHUNK)
        taps = [wdw_ref[k, :, cols] for k in range(FFN_CONV_WIDTH)]
        gc = _dwconv_rows(ge3, taps, FFN_CONV_WIDTH // 2, n, sub) + bdw_ref[:, cols]
        acts.append((_gelu_tanh(gc) * val.reshape(n, V7X_SUBLANES, FFN_CHUNK)).reshape(tm, FFN_CHUNK).astype(BF16))
    act = jnp.concatenate(acts, axis=1)

    rb = tm // FFN_TAIL_BLOCKS
    for q in range(FFN_TAIL_BLOCKS):
        rows = slice(q * rb, (q + 1) * rb)
        y = jnp.dot(act[rows], wd_ref[...], preferred_element_type=F32)
        o_ref[rows] = _post_residual(x[rows], y, postg_ref[...], gate_ref[...])


def _proj_body(x_ref, gain_ref, shift_ref, scale_ref, w_ref, *o_refs, col0):
    h = _modnorm(x_ref[...], gain_ref[...], shift_ref[...], scale_ref[...]).astype(BF16)
    for t, o_ref in enumerate(o_refs):
        cols = slice(col0 + t * D_MODEL, col0 + (t + 1) * D_MODEL)
        o_ref[...] = jnp.dot(h, w_ref[:, cols], preferred_element_type=F32).astype(o_ref.dtype)


def _bias_body(src_ref, o_ref):
    qc = lax.broadcasted_iota(jnp.int32, (GRID_W, ATTN_PAIR), 0)
    kc = lax.broadcasted_iota(jnp.int32, (GRID_W, ATTN_PAIR), 1) & (GRID_W - 1)
    c0 = jnp.clip(qc - NA_KW // 2, 0, GRID_W - NA_KW)
    in_win = (kc >= c0) & (kc < c0 + NA_KW)
    base_shift = V7X_LANES - (NA_KW - 1)
    for s in range(2 * NA_KH - 2):
        for e in range(2):
            src = jnp.broadcast_to(src_ref[s, e], (GRID_W, ATTN_PAIR))
            tile = pltpu.roll(src, base_shift, 1, stride=1, stride_axis=0)
            o_ref[s, e * GRID_W:(e + 1) * GRID_W, :] = jnp.where(in_win, tile, NEG_INF)


def _out_proj_body(a_ref, x_ref, w_ref, postg_ref, gate_ref, o_ref):
    y = jnp.dot(a_ref[...], w_ref[...], preferred_element_type=F32)
    o_ref[...] = _post_residual(x_ref[...], y, postg_ref[...], gate_ref[...])


def _attn_body(q_ref, k_ref, v_ref, kc_ref, vc_ref, bias_ref, o_ref, *, rows, rows_per_step):
    n_loc = NA_KH * GRID_W
    lo = lax.broadcasted_iota(jnp.int32, (GRID_W, ATTN_PAIR), 1) < NA_HEAD_DIM
    nt = (((1,), (1,)), ((), ()))

    def scores(p, rr):
        lanes = slice(p * ATTN_PAIR, (p + 1) * ATTN_PAIR)
        r = pl.program_id(1) * rows_per_step + rr
        r0 = jnp.clip(r - NA_KH // 2, 0, rows - NA_KH)
        row_off = r0 - r + (NA_KH - 1)
        start = pl.multiple_of(r0 * GRID_W, GRID_W)
        qp = q_ref[rr * GRID_W:(rr + 1) * GRID_W, lanes]
        zero = jnp.zeros_like(qp)
        qs = jnp.concatenate([jnp.where(lo, qp, zero), jnp.where(lo, zero, qp)], axis=0)
        bias = jnp.concatenate([bias_ref[p, row_off + 2 * jj] for jj in range(NA_KH // 2)], axis=1)
        k_all = jnp.concatenate([k_ref[pl.ds(start, n_loc), lanes], kc_ref[:, lanes]], axis=0)
        s = lax.dot_general(qs, k_all, nt, preferred_element_type=F32)
        return s[:, :n_loc] + bias, s[:, n_loc:], start

    units = [(p, rr) for p in range(NA_HEADS // 2) for rr in range(rows_per_step)]
    pending = [scores(*units[i]) for i in range(ATTN_LOOKAHEAD)]
    for idx, (p, rr) in enumerate(units):
        s_loc, s_ctx, start = pending.pop(0)
        if idx + ATTN_LOOKAHEAD < len(units):
            pending.append(scores(*units[idx + ATTN_LOOKAHEAD]))
        lanes = slice(p * ATTN_PAIR, (p + 1) * ATTN_PAIR)
        m = jnp.maximum(jnp.max(s_loc, axis=-1, keepdims=True), jnp.max(s_ctx, axis=-1, keepdims=True))
        probs = jnp.concatenate([jnp.exp(s_loc - m).astype(BF16), jnp.exp(s_ctx - m).astype(BF16)], axis=1)
        v_all = jnp.concatenate([v_ref[pl.ds(start, n_loc), lanes], vc_ref[:, lanes]], axis=0)
        v_ext = jnp.concatenate([v_all, jnp.ones_like(v_all)], axis=1)
        o_ext = jnp.dot(probs, v_ext, preferred_element_type=F32)
        o = o_ext[:, :ATTN_PAIR] / o_ext[:, ATTN_PAIR:ATTN_PAIR + 1]
        o_ref[rr * GRID_W:(rr + 1) * GRID_W, lanes] = jnp.where(lo, o[:GRID_W], o[GRID_W:]).astype(o_ref.dtype)


def _tile_rows(t, pref=512):
    return pref if t % pref == 0 else t


def _row_spec(tm, width=D_MODEL):
    return pl.BlockSpec((None, tm, width), lambda b, i: (b, i, 0))


def _halo_specs(tm, t):
    per = tm // HALO_ROWS
    last = t // HALO_ROWS - 1
    prev = pl.BlockSpec((None, HALO_ROWS, D_MODEL), lambda b, i: (b, jnp.maximum(i * per - 1, 0), 0))
    nxt = pl.BlockSpec((None, HALO_ROWS, D_MODEL), lambda b, i: (b, jnp.minimum((i + 1) * per, last), 0))
    return prev, nxt


def _const_spec(shape):
    nd = len(shape)
    return pl.BlockSpec(shape, lambda b, i: (0,) * nd, pipeline_mode=pl.Buffered(1))


def _vec_spec():
    return pl.BlockSpec((1, D_MODEL), lambda b, i: (0, 0))


def _mod_spec(layer, comp, ctx):
    if ctx is None:
        return pl.BlockSpec((None, None, None, 1, D_MODEL), lambda b, i: (layer, comp, b, 0, 0))
    return pl.BlockSpec((None, None, None, 1, D_MODEL), lambda b, i: (layer, comp, ctx, 0, 0))


def _params(nbytes):
    return pltpu.CompilerParams(dimension_semantics=("arbitrary", "arbitrary"), vmem_limit_bytes=_vmem_limit(nbytes))


def _ada_mod(cs, mod_w, mod_b):
    depth = mod_w.shape[0]
    blk = _nbytes((D_MODEL, D_MODEL), F32)
    return pl.pallas_call(
        _ada_body,
        grid=(depth, N_MOD),
        in_specs=[
            pl.BlockSpec((MOD_ROWS, D_MODEL), lambda l, k: (0, 0)),
            pl.BlockSpec((None, D_MODEL, D_MODEL), lambda l, k: (l, 0, k)),
            pl.BlockSpec((None, 1, D_MODEL), lambda l, k: (l, 0, k)),
        ],
        out_specs=pl.BlockSpec((None, None, MOD_ROWS, 1, D_MODEL), lambda l, k: (l, k, 0, 0, 0)),
        out_shape=jax.ShapeDtypeStruct((depth, N_MOD, MOD_ROWS, 1, D_MODEL), F32),
        compiler_params=pltpu.CompilerParams(dimension_semantics=("arbitrary", "arbitrary"),
                                             vmem_limit_bytes=_vmem_limit(4 * blk)),
        name="ada_mod",
    )(cs, mod_w, mod_b.reshape(depth, 1, N_MOD * D_MODEL))


def _pw1_glu(x, gain, mods, layer, ctx, w_bf, b):
    bsz, t, _ = x.shape
    tm = _tile_rows(t, PROJ_ROWS)
    est = 4 * _nbytes((tm, D_MODEL), F32) + _nbytes(w_bf.shape, BF16) + 3 * _nbytes((tm, 2 * D_MODEL), F32)
    return pl.pallas_call(
        functools.partial(_pw1_glu_body, tm=tm),
        grid=(bsz, t // tm),
        in_specs=[_row_spec(tm), _vec_spec(), _mod_spec(layer, 0, ctx), _mod_spec(layer, 1, ctx),
                  _const_spec(w_bf.shape), _const_spec(b.shape)],
        out_specs=_row_spec(tm),
        out_shape=jax.ShapeDtypeStruct((bsz, t, D_MODEL), F32),
        compiler_params=_params(est),
        name="pw1_glu",
    )(x, gain, mods, mods, w_bf, b)


def _conv_mix(u, x, w_dw, b_dw, ln_g, ln_b, w2_bf, b2, post_g, mods, layer, ctx):
    bsz, t, _ = x.shape
    tm = _tile_rows(t, PROJ_ROWS)
    n_tiles = t // tm
    prev, nxt = _halo_specs(tm, t)
    est = (8 * _nbytes((tm, D_MODEL), F32) + _nbytes(w2_bf.shape, BF16) + _nbytes(w_dw.shape, F32)
           + 4 * _nbytes((tm, D_MODEL), F32))
    return pl.pallas_call(
        functools.partial(_conv_mix_body, tm=tm, n_tiles=n_tiles),
        grid=(bsz, n_tiles),
        in_specs=[prev, _row_spec(tm), nxt, _row_spec(tm),
                  _const_spec(w_dw.shape), _const_spec(b_dw.shape), _vec_spec(), _vec_spec(),
                  _const_spec(w2_bf.shape), _vec_spec(), _vec_spec(), _mod_spec(layer, 2, ctx)],
        out_specs=_row_spec(tm),
        out_shape=jax.ShapeDtypeStruct((bsz, t, D_MODEL), F32),
        scratch_shapes=[pltpu.VMEM((D_MODEL // V7X_LANES, tm + 2 * HALO_ROWS, V7X_LANES), F32),
                        pltpu.VMEM((tm, D_MODEL), F32)],
        compiler_params=_params(est),
        name="conv_mix",
    )(u, u, u, x, w_dw, b_dw, ln_g, ln_b, w2_bf, b2, post_g, mods)


def _layer_spec(stacked_shape, layer):
    nd = len(stacked_shape) - 1
    return pl.BlockSpec((None,) + tuple(stacked_shape[1:]), lambda b, i: (layer,) + (0,) * nd,
                        pipeline_mode=pl.Buffered(1))


def _ffn(x, gain, post_g, mods, layer, ctx, wup, w_dw, b_dw, wd):
    bsz, t, _ = x.shape
    tm = _tile_rows(t, FFN_ROWS)
    n_tiles = t // tm
    prev, nxt = _halo_specs(tm, t)
    est = (8 * _nbytes((tm, D_MODEL), F32) + _nbytes(wup.shape[1:], BF16) + _nbytes(wd.shape[1:], BF16)
           + _nbytes((tm, D_FF), BF16) + 8 * _nbytes((tm + 2 * HALO_ROWS, FFN_CHUNK), F32))
    return pl.pallas_call(
        functools.partial(_ffn_body, tm=tm, n_tiles=n_tiles),
        grid=(bsz, n_tiles),
        in_specs=[prev, _row_spec(tm), nxt, _vec_spec(), _mod_spec(layer, 3, ctx), _mod_spec(layer, 4, ctx),
                  _layer_spec(wup.shape, layer), _const_spec(w_dw.shape), _const_spec(b_dw.shape),
                  _layer_spec(wd.shape, layer), _vec_spec(), _mod_spec(layer, 5, ctx)],
        out_specs=_row_spec(tm),
        out_shape=jax.ShapeDtypeStruct((bsz, t, D_MODEL), F32),
        compiler_params=_params(est),
        name="conv_ffn",
    )(x, x, x, gain, mods, mods, wup, w_dw, b_dw, wd, post_g, mods)


def _proj(x, gain, mods, layer, ctx, w_bf, col0, n_out):
    bsz, t, _ = x.shape
    tm = _tile_rows(t, PROJ_ROWS)
    est = 2 * _nbytes((tm, D_MODEL), F32) + _nbytes(w_bf.shape, BF16) + (2 * n_out + 4) * _nbytes((tm, D_MODEL), F32)
    return pl.pallas_call(
        functools.partial(_proj_body, col0=col0),
        grid=(bsz, t // tm),
        in_specs=[_row_spec(tm), _vec_spec(), _mod_spec(layer, 0, ctx), _mod_spec(layer, 1, ctx),
                  _const_spec(w_bf.shape)],
        out_specs=[_row_spec(tm)] * n_out,
        out_shape=[jax.ShapeDtypeStruct((bsz, t, D_MODEL), BF16)] * n_out,
        compiler_params=_params(est),
        name="norm_proj",
    )(x, gain, mods, mods, w_bf)


def _out_proj(a, x, w_bf, post_g, mods, layer):
    bsz, t, _ = x.shape
    tm = _tile_rows(t, PROJ_ROWS)
    est = 6 * _nbytes((tm, D_MODEL), F32) + _nbytes(w_bf.shape, BF16)
    return pl.pallas_call(
        _out_proj_body,
        grid=(bsz, t // tm),
        in_specs=[_row_spec(tm), _row_spec(tm), _const_spec(w_bf.shape), _vec_spec(), _mod_spec(layer, 2, None)],
        out_specs=_row_spec(tm),
        out_shape=jax.ShapeDtypeStruct((bsz, t, D_MODEL), F32),
        compiler_params=_params(est),
        name="out_proj",
    )(a, x, w_bf, post_g, mods)


def _attention(q, k, v, kc, vc, bias):
    bsz, t, _ = q.shape
    rows = t // GRID_W
    c_len = kc.shape[1]
    rps = ATTN_ROWS_PER_STEP
    assert rows % rps == 0
    tq = rps * GRID_W
    est = (4 * _nbytes((t, D_MODEL), BF16) + 4 * _nbytes((c_len, D_MODEL), BF16)
           + _nbytes(bias.shape, F32) + 4 * _nbytes((tq, D_MODEL), BF16))
    return pl.pallas_call(
        functools.partial(_attn_body, rows=rows, rows_per_step=rps),
        grid=(bsz, rows // rps),
        in_specs=[
            pl.BlockSpec((None, tq, D_MODEL), lambda b, r: (b, r, 0)),
            pl.BlockSpec((None, t, D_MODEL), lambda b, r: (b, 0, 0)),
            pl.BlockSpec((None, t, D_MODEL), lambda b, r: (b, 0, 0)),
            pl.BlockSpec((None, c_len, D_MODEL), lambda b, r: (b, 0, 0)),
            pl.BlockSpec((None, c_len, D_MODEL), lambda b, r: (b, 0, 0)),
            _const_spec(bias.shape),
        ],
        out_specs=pl.BlockSpec((None, tq, D_MODEL), lambda b, r: (b, r, 0)),
        out_shape=jax.ShapeDtypeStruct((bsz, t, D_MODEL), BF16),
        compiler_params=_params(est),
        name="na_attention",
    )(q, k, v, kc, vc, bias)


def _attn_bias_table(rpb):
    n_pairs = NA_HEADS // 2
    n_s = 2 * NA_KH - 2
    rp = jnp.pad(rpb.astype(F32), ((0, 0), (0, 0), (0, GRID_W - (2 * NA_KW - 1))))
    src = jnp.concatenate([rp[:, :n_s], rp[:, 1:]], axis=-1)
    src = jnp.transpose(src.reshape(n_pairs, 2, n_s, ATTN_PAIR), (0, 2, 1, 3))[:, :, :, None, :]
    return pl.pallas_call(
        _bias_body,
        grid=(n_pairs,),
        in_specs=[pl.BlockSpec((None, n_s, 2, 1, ATTN_PAIR), lambda p: (p, 0, 0, 0, 0))],
        out_specs=pl.BlockSpec((None, n_s, 2 * GRID_W, ATTN_PAIR), lambda p: (p, 0, 0, 0)),
        out_shape=jax.ShapeDtypeStruct((n_pairs, n_s, 2 * GRID_W, ATTN_PAIR), F32),
        name="na_bias",
    )(src)


def kernel(x, c, ctx, c_ctx, mod_w, mod_b, mix_pre_g, mix_post_g, ffn_pre_g, ffn_post_g, cv_w_pw1, cv_b_pw1, cv_w_dw, cv_b_dw, cv_ln_g, cv_ln_b, cv_w_pw2, cv_b_pw2, na_w_qkv, na_w_o, na_rpb, ffn_w_up, ffn_w_dw, ffn_b_dw, ffn_w_down):
    bsz = x.shape[0]
    assert bsz + 1 <= MOD_ROWS and x.shape[2] == D_MODEL and mod_w.shape[0] == DEPTH
    ctx_row = bsz
    cs = jnp.concatenate([c, c_ctx[None, :], jnp.zeros((MOD_ROWS - bsz - 1, D_MODEL), F32)], axis=0)
    mods = _ada_mod(cs, mod_w, mod_b)
    wup_all = ffn_w_up.astype(BF16)
    wd_all = ffn_w_down.astype(BF16)

    h_ctx = ctx
    for i in range(DEPTH):
        last = i == DEPTH - 1
        j = i // 2
        use_na = (i % 2) == 1
        pre_g = mix_pre_g[i][None, :]
        post_g = mix_post_g[i][None, :]
        ffn = (wup_all, jnp.broadcast_to(ffn_w_dw[i][:, None, :], (FFN_CONV_WIDTH, V7X_SUBLANES, D_FF)),
               jnp.broadcast_to(ffn_b_dw[i][None, :], (V7X_SUBLANES, D_FF)), wd_all)
        if use_na:
            scale = NA_HEAD_DIM ** -0.5
            col_scale = jnp.concatenate([jnp.full((D_MODEL,), scale, F32), jnp.ones((2 * D_MODEL,), F32)])
            w_qkv = (na_w_qkv[j] * col_scale[None, :]).astype(BF16)
            q, k, v = _proj(x, pre_g, mods, i, None, w_qkv, 0, 3)
            kc, vc = _proj(h_ctx, pre_g, mods, i, ctx_row, w_qkv, D_MODEL, 2)
            attn = _attention(q, k, v, kc, vc, _attn_bias_table(na_rpb[j]))
            x = _out_proj(attn, x, na_w_o[j].astype(BF16), post_g, mods, i)
            assert last, "context queries (an attention layer that is not the last) are not implemented"
        else:
            w1 = cv_w_pw1[j].astype(BF16)
            b1 = cv_b_pw1[j][None, :]
            w_dw = jnp.broadcast_to(cv_w_dw[j][:, None, :], (CONV_WIDTH, V7X_SUBLANES, D_MODEL))
            b_dw = jnp.broadcast_to(cv_b_dw[j][None, :], (V7X_SUBLANES, D_MODEL))
            w2 = cv_w_pw2[j].astype(BF16)
            cv = (w_dw, b_dw, cv_ln_g[j][None, :], cv_ln_b[j][None, :], w2, cv_b_pw2[j][None, :], post_g)
            u = _pw1_glu(x, pre_g, mods, i, None, w1, b1)
            x = _conv_mix(u, x, *cv, mods, i, None)
            if not last:
                u_ctx = _pw1_glu(h_ctx, pre_g, mods, i, ctx_row, w1, b1)
                h_ctx = _conv_mix(u_ctx, h_ctx, *cv, mods, i, ctx_row)
        x = _ffn(x, ffn_pre_g[i][None, :], ffn_post_g[i][None, :], mods, i, None, *ffn)
        if not last:
            h_ctx = _ffn(h_ctx, ffn_pre_g[i][None, :], ffn_post_g[i][None, :], mods, i, ctx_row, *ffn)
    return x
```

```python
import functools
import math

import jax
import jax.numpy as jnp
from jax import lax
from jax.experimental import pallas as pl
from jax.experimental.pallas import tpu as pltpu

D_MODEL = 1024
DEPTH = 2
GRID_W = 64
CONV_WIDTH = 31
NA_HEADS = 16
NA_HEAD_DIM = D_MODEL // NA_HEADS
NA_KH = 8
NA_KW = 16
D_FF = 2816
FFN_CONV_WIDTH = 3
N_MOD = 6
RMS_EPS = 1e-6
LN_EPS = 1e-5
NEG_INF = -1e30

V7X_LANES = 128
V7X_SUBLANES = 8
V7X_VMEM_BYTES = 64 * 1024 * 1024

HALO_ROWS = 2 * V7X_SUBLANES
MOD_ROWS = 16
FFN_CHUNK = 256
FFN_ROWS = 512
FFN_TAIL_BLOCKS = 2
CONV_ROWS = 128
PROJ_ROWS = 1024
ATTN_PAIR = 2 * NA_HEAD_DIM
ATTN_ROWS_PER_STEP = 4
ATTN_LOOKAHEAD = 3

F32 = jnp.float32
BF16 = jnp.bfloat16


def _vmem_limit(nbytes):
    return int(min(V7X_VMEM_BYTES - 6 * 1024 * 1024, max(2 * nbytes, 24 * 1024 * 1024)))


def _nbytes(shape, dtype):
    return math.prod(shape) * jnp.dtype(dtype).itemsize


def _modnorm(xf, gain, shift, scale):
    ms = jnp.mean(xf * xf, axis=-1, keepdims=True)
    return (xf * lax.rsqrt(ms + RMS_EPS)) * (gain * (1.0 + scale)) + shift


def _post_residual(xf, y, post_g, gate):
    ms = jnp.mean(y * y, axis=-1, keepdims=True)
    return xf + gate * ((y * lax.rsqrt(ms + RMS_EPS)) * post_g)


def _gelu_tanh(x):
    c = math.sqrt(2.0 / math.pi)
    return x * (0.5 * (1.0 + jnp.tanh(c * (x + 0.044715 * (x * x * x)))))


def _dwconv_rows(xe3, taps, pad, n, sub):
    out = None
    for r in range(V7X_SUBLANES):
        offs = [d for d in range(-pad, pad + 1) if d % V7X_SUBLANES == r]
        if not offs:
            continue
        rr = xe3 if r == 0 else pltpu.roll(xe3, V7X_SUBLANES - r, axis=1)
        m = n if r == 0 else n + 1
        part = None
        for d in offs:
            a = d // V7X_SUBLANES
            term = taps[d + pad] * rr[2 + a: 2 + a + m]
            part = term if part is None else part + term
        if r != 0:
            part = jnp.where(sub < V7X_SUBLANES - r, part[0:n], part[1:n + 1])
        out = part if out is None else out + part
    return out


def _ada_body(cs_ref, w_ref, b_ref, o_ref):
    s = cs_ref[...]
    s = s * jax.nn.sigmoid(s)
    m = jnp.dot(s.astype(BF16), w_ref[...].astype(BF16), preferred_element_type=F32) + b_ref[...]
    for row in range(MOD_ROWS):
        o_ref[row] = m[row:row + 1, :]


def _pw1_glu_body(x_ref, gain_ref, shift_ref, scale_ref, w_ref, b_ref, o_ref, *, tm):
    gain, shift, scale = gain_ref[...], shift_ref[...], scale_ref[...]
    rb = tm // 2
    hs = [_modnorm(x_ref[q * rb:(q + 1) * rb], gain, shift, scale).astype(BF16) for q in range(2)]
    for q, h in enumerate(hs):
        a = jnp.dot(h, w_ref[:, :D_MODEL], preferred_element_type=F32) + b_ref[:, :D_MODEL]
        g = jnp.dot(h, w_ref[:, D_MODEL:], preferred_element_type=F32) + b_ref[:, D_MODEL:]
        o_ref[q * rb:(q + 1) * rb] = a * jax.nn.sigmoid(g)


def _conv_mix_body(up_ref, u_ref, un_ref, x_ref, wdw_ref, bdw_ref, lng_ref, lnb_ref, w2_ref, b2_ref,
                   postg_ref, gate_ref, o_ref, ue_scr, cv_scr, *, tm, n_tiles):
    i = pl.program_id(1)
    for s in range(D_MODEL // V7X_LANES):
        lanes = slice(s * V7X_LANES, (s + 1) * V7X_LANES)
        ue_scr[s, 0:HALO_ROWS] = jnp.where(i > 0, up_ref[:, lanes], 0.0)
        ue_scr[s, HALO_ROWS:HALO_ROWS + tm] = u_ref[:, lanes]
        ue_scr[s, HALO_ROWS + tm:] = jnp.where(i < n_tiles - 1, un_ref[:, lanes], 0.0)

    first = HALO_ROWS - CONV_WIDTH // 2

    def conv_step(ci, carry):
        base = pl.multiple_of(ci * CONV_ROWS, CONV_ROWS)
        for s in range(D_MODEL // V7X_LANES):
            lanes = slice(s * V7X_LANES, (s + 1) * V7X_LANES)
            acc = None
            for k in range(CONV_WIDTH):
                shifted = ue_scr[s, pl.ds(base + first + k, CONV_ROWS, stride=1), :]
                term = wdw_ref[k, 0:1, lanes] * shifted
                acc = term if acc is None else acc + term
            cv_scr[pl.ds(base, CONV_ROWS), lanes] = acc + bdw_ref[0:1, lanes]
        return carry

    lax.fori_loop(0, tm // CONV_ROWS, conv_step, 0)

    u = cv_scr[...]
    uc = u - jnp.mean(u, axis=-1, keepdims=True)
    ln = (uc * lax.rsqrt(jnp.mean(uc * uc, axis=-1, keepdims=True) + LN_EPS)) * lng_ref[...] + lnb_ref[...]
    act = (ln * jax.nn.sigmoid(ln)).astype(BF16)
    y = jnp.dot(act, w2_ref[...], preferred_element_type=F32) + b2_ref[...]
    o_ref[...] = _post_residual(x_ref[...], y, postg_ref[...], gate_ref[...])


def _ffn_body(xp_ref, x_ref, xn_ref, gain_ref, shift_ref, scale_ref, wup_ref, wdw_ref, bdw_ref, wd_ref,
              postg_ref, gate_ref, o_ref, *, tm, n_tiles):
    i = pl.program_id(1)
    gain, shift, scale = gain_ref[...], shift_ref[...], scale_ref[...]
    x = x_ref[...]
    hp = jnp.where(i > 0, _modnorm(xp_ref[...], gain, shift, scale), 0.0).astype(BF16)
    hn = jnp.where(i < n_tiles - 1, _modnorm(xn_ref[...], gain, shift, scale), 0.0).astype(BF16)
    half = tm // 2
    h_lo = _modnorm(x[:half], gain, shift, scale).astype(BF16)
    h_hi = _modnorm(x[half:], gain, shift, scale).astype(BF16)
    h = jnp.concatenate([h_lo, h_hi], axis=0)
    he = jnp.concatenate([hp, h, hn], axis=0)

    n = tm // V7X_SUBLANES
    sub = lax.broadcasted_iota(jnp.int32, (n, V7X_SUBLANES, FFN_CHUNK), 1)
    n_chunks = D_FF // FFN_CHUNK

    def up_proj(j):
        cols = slice(j * FFN_CHUNK, (j + 1) * FFN_CHUNK)
        vcols = slice(D_FF + j * FFN_CHUNK, D_FF + (j + 1) * FFN_CHUNK)
        if j == 0:
            lo_rows = HALO_ROWS + half
            ge_lo = jnp.dot(he[:lo_rows], wup_ref[:, cols], preferred_element_type=F32)
            val_lo = jnp.dot(h_lo, wup_ref[:, vcols], preferred_element_type=F32)
            ge_hi = jnp.dot(he[lo_rows:], wup_ref[:, cols], preferred_element_type=F32)
            val_hi = jnp.dot(h_hi, wup_ref[:, vcols], preferred_element_type=F32)
            return jnp.concatenate([ge_lo, ge_hi], axis=0), jnp.concatenate([val_lo, val_hi], axis=0)
        ge = jnp.dot(he, wup_ref[:, cols], preferred_element_type=F32)
        val = jnp.dot(h, wup_ref[:, vcols], preferred_element_type=F32)
        return ge, val

    acts = []
    nxt = up_proj(0)
    for j in range(n_chunks):
        ge, val = nxt
        if j + 1 < n_chunks:
            nxt = up_proj(j + 1)
        cols = slice(j * FFN_CHUNK, (j + 1) * FFN_CHUNK)
        ge3 = ge.reshape(n + 4, V7X_SUBLANES, FFN_CHUNK)
        taps = [wdw_ref[k, :, cols] for k in range(FFN_CONV_WIDTH)]
        gc = _dwconv_rows(ge3, taps, FFN_CONV_WIDTH // 2, n, sub) + bdw_ref[:, cols]
        acts.append((_gelu_tanh(gc) * val.reshape(n, V7X_SUBLANES, FFN_CHUNK)).reshape(tm, FFN_CHUNK).astype(BF16))
    act = jnp.concatenate(acts, axis=1)

    rb = tm // FFN_TAIL_BLOCKS
    wd = wd_ref[...].astype(BF16)
    for q in range(FFN_TAIL_BLOCKS):
        rows = slice(q * rb, (q + 1) * rb)
        y = jnp.dot(act[rows], wd, preferred_element_type=F32)
        o_ref[rows] = _post_residual(x[rows], y, postg_ref[...], gate_ref[...])


def _proj_body(x_ref, gain_ref, shift_ref, scale_ref, w_ref, *o_refs, col0):
    h = _modnorm(x_ref[...], gain_ref[...], shift_ref[...], scale_ref[...]).astype(BF16)
    for t, o_ref in enumerate(o_refs):
        cols = slice(col0 + t * D_MODEL, col0 + (t + 1) * D_MODEL)
        o_ref[...] = jnp.dot(h, w_ref[:, cols], preferred_element_type=F32).astype(o_ref.dtype)


def _bias_body(src_ref, o_ref):
    qc = lax.broadcasted_iota(jnp.int32, (GRID_W, ATTN_PAIR), 0)
    kc = lax.broadcasted_iota(jnp.int32, (GRID_W, ATTN_PAIR), 1) & (GRID_W - 1)
    c0 = jnp.clip(qc - NA_KW // 2, 0, GRID_W - NA_KW)
    in_win = (kc >= c0) & (kc < c0 + NA_KW)
    base_shift = V7X_LANES - (NA_KW - 1)
    for s in range(2 * NA_KH - 2):
        for e in range(2):
            src = jnp.broadcast_to(src_ref[s, e], (GRID_W, ATTN_PAIR))
            tile = pltpu.roll(src, base_shift, 1, stride=1, stride_axis=0)
            o_ref[s, e * GRID_W:(e + 1) * GRID_W, :] = jnp.where(in_win, tile, NEG_INF)


def _out_proj_body(a_ref, x_ref, w_ref, postg_ref, gate_ref, o_ref):
    y = jnp.dot(a_ref[...], w_ref[...], preferred_element_type=F32)
    o_ref[...] = _post_residual(x_ref[...], y, postg_ref[...], gate_ref[...])


def _attn_body(q_ref, k_ref, v_ref, kc_ref, vc_ref, bias_ref, o_ref, *, rows, rows_per_step):
    n_loc = NA_KH * GRID_W
    lo = lax.broadcasted_iota(jnp.int32, (GRID_W, ATTN_PAIR), 1) < NA_HEAD_DIM
    nt = (((1,), (1,)), ((), ()))

    def scores(p, rr):
        lanes = slice(p * ATTN_PAIR, (p + 1) * ATTN_PAIR)
        r = pl.program_id(1) * rows_per_step + rr
        r0 = jnp.clip(r - NA_KH // 2, 0, rows - NA_KH)
        row_off = r0 - r + (NA_KH - 1)
        start = pl.multiple_of(r0 * GRID_W, GRID_W)
        qp = q_ref[rr * GRID_W:(rr + 1) * GRID_W, lanes]
        zero = jnp.zeros_like(qp)
        qs = jnp.concatenate([jnp.where(lo, qp, zero), jnp.where(lo, zero, qp)], axis=0)
        bias = jnp.concatenate([bias_ref[p, row_off + 2 * jj] for jj in range(NA_KH // 2)], axis=1)
        k_all = jnp.concatenate([k_ref[pl.ds(start, n_loc), lanes], kc_ref[:, lanes]], axis=0)
        s = lax.dot_general(qs, k_all, nt, preferred_element_type=F32)
        return s[:, :n_loc] + bias, s[:, n_loc:], start

    units = [(p, rr) for p in range(NA_HEADS // 2) for rr in range(rows_per_step)]
    pending = [scores(*units[i]) for i in range(ATTN_LOOKAHEAD)]
    for idx, (p, rr) in enumerate(units):
        s_loc, s_ctx, start = pending.pop(0)
        if idx + ATTN_LOOKAHEAD < len(units):
            pending.append(scores(*units[idx + ATTN_LOOKAHEAD]))
        lanes = slice(p * ATTN_PAIR, (p + 1) * ATTN_PAIR)
        m = jnp.maximum(jnp.max(s_loc, axis=-1, keepdims=True), jnp.max(s_ctx, axis=-1, keepdims=True))
        probs = jnp.concatenate([jnp.exp(s_loc - m).astype(BF16), jnp.exp(s_ctx - m).astype(BF16)], axis=1)
        v_all = jnp.concatenate([v_ref[pl.ds(start, n_loc), lanes], vc_ref[:, lanes]], axis=0)
        v_ext = jnp.concatenate([v_all, jnp.ones_like(v_all)], axis=1)
        o_ext = jnp.dot(probs, v_ext, preferred_element_type=F32)
        o = o_ext[:, :ATTN_PAIR] / o_ext[:, ATTN_PAIR:ATTN_PAIR + 1]
        o_ref[rr * GRID_W:(rr + 1) * GRID_W, lanes] = jnp.where(lo, o[:GRID_W], o[GRID_W:]).astype(o_ref.dtype)


def _tile_rows(t, pref=512):
    return pref if t % pref == 0 else t


def _row_spec(tm, width=D_MODEL):
    return pl.BlockSpec((None, tm, width), lambda b, i: (b, i, 0))


def _halo_specs(tm, t):
    per = tm // HALO_ROWS
    last = t // HALO_ROWS - 1
    prev = pl.BlockSpec((None, HALO_ROWS, D_MODEL), lambda b, i: (b, jnp.maximum(i * per - 1, 0), 0))
    nxt = pl.BlockSpec((None, HALO_ROWS, D_MODEL), lambda b, i: (b, jnp.minimum((i + 1) * per, last), 0))
    return prev, nxt


def _const_spec(shape):
    nd = len(shape)
    return pl.BlockSpec(shape, lambda b, i: (0,) * nd, pipeline_mode=pl.Buffered(1))


def _vec_spec():
    return pl.BlockSpec((1, D_MODEL), lambda b, i: (0, 0))


def _mod_spec(layer, comp, ctx):
    if ctx is None:
        return pl.BlockSpec((None, None, None, 1, D_MODEL), lambda b, i: (layer, comp, b, 0, 0))
    return pl.BlockSpec((None, None, None, 1, D_MODEL), lambda b, i: (layer, comp, ctx, 0, 0))


def _params(nbytes):
    return pltpu.CompilerParams(dimension_semantics=("arbitrary", "arbitrary"), vmem_limit_bytes=_vmem_limit(nbytes))


def _ada_mod(cs, mod_w, mod_b):
    depth = mod_w.shape[0]
    blk = _nbytes((D_MODEL, D_MODEL), F32)
    return pl.pallas_call(
        _ada_body,
        grid=(depth, N_MOD),
        in_specs=[
            pl.BlockSpec((MOD_ROWS, D_MODEL), lambda l, k: (0, 0)),
            pl.BlockSpec((None, D_MODEL, D_MODEL), lambda l, k: (l, 0, k)),
            pl.BlockSpec((None, 1, D_MODEL), lambda l, k: (l, 0, k)),
        ],
        out_specs=pl.BlockSpec((None, None, MOD_ROWS, 1, D_MODEL), lambda l, k: (l, k, 0, 0, 0)),
        out_shape=jax.ShapeDtypeStruct((depth, N_MOD, MOD_ROWS, 1, D_MODEL), F32),
        compiler_params=pltpu.CompilerParams(dimension_semantics=("arbitrary", "arbitrary"),
                                             vmem_limit_bytes=_vmem_limit(4 * blk)),
        name="ada_mod",
    )(cs, mod_w, mod_b.reshape(depth, 1, N_MOD * D_MODEL))


def _pw1_glu(x, gain, mods, layer, ctx, w_bf, b):
    bsz, t, _ = x.shape
    tm = _tile_rows(t, PROJ_ROWS)
    est = 4 * _nbytes((tm, D_MODEL), F32) + _nbytes(w_bf.shape, BF16) + 3 * _nbytes((tm, 2 * D_MODEL), F32)
    return pl.pallas_call(
        functools.partial(_pw1_glu_body, tm=tm),
        grid=(bsz, t // tm),
        in_specs=[_row_spec(tm), _vec_spec(), _mod_spec(layer, 0, ctx), _mod_spec(layer, 1, ctx),
                  _const_spec(w_bf.shape), _const_spec(b.shape)],
        out_specs=_row_spec(tm),
        out_shape=jax.ShapeDtypeStruct((bsz, t, D_MODEL), F32),
        compiler_params=_params(est),
        name="pw1_glu",
    )(x, gain, mods, mods, w_bf, b)


def _conv_mix(u, x, w_dw, b_dw, ln_g, ln_b, w2_bf, b2, post_g, mods, layer, ctx):
    bsz, t, _ = x.shape
    tm = _tile_rows(t)
    n_tiles = t // tm
    prev, nxt = _halo_specs(tm, t)
    est = (8 * _nbytes((tm, D_MODEL), F32) + _nbytes(w2_bf.shape, BF16) + _nbytes(w_dw.shape, F32)
           + 4 * _nbytes((tm, D_MODEL), F32))
    return pl.pallas_call(
        functools.partial(_conv_mix_body, tm=tm, n_tiles=n_tiles),
        grid=(bsz, n_tiles),
        in_specs=[prev, _row_spec(tm), nxt, _row_spec(tm),
                  _const_spec(w_dw.shape), _const_spec(b_dw.shape), _vec_spec(), _vec_spec(),
                  _const_spec(w2_bf.shape), _vec_spec(), _vec_spec(), _mod_spec(layer, 2, ctx)],
        out_specs=_row_spec(tm),
        out_shape=jax.ShapeDtypeStruct((bsz, t, D_MODEL), F32),
        scratch_shapes=[pltpu.VMEM((D_MODEL // V7X_LANES, tm + 2 * HALO_ROWS, V7X_LANES), F32),
                        pltpu.VMEM((tm, D_MODEL), F32)],
        compiler_params=_params(est),
        name="conv_mix",
    )(u, u, u, x, w_dw, b_dw, ln_g, ln_b, w2_bf, b2, post_g, mods)


def _layer_spec(stacked_shape, layer):
    nd = len(stacked_shape) - 1
    return pl.BlockSpec((None,) + tuple(stacked_shape[1:]), lambda b, i: (layer,) + (0,) * nd,
                        pipeline_mode=pl.Buffered(1))


def _ffn(x, gain, post_g, mods, layer, ctx, wup, w_dw, b_dw, wd):
    bsz, t, _ = x.shape
    tm = _tile_rows(t, FFN_ROWS)
    n_tiles = t // tm
    prev, nxt = _halo_specs(tm, t)
    est = (8 * _nbytes((tm, D_MODEL), F32) + _nbytes(wup.shape[1:], BF16) + _nbytes(wd.shape[1:], F32) + _nbytes(wd.shape[1:], BF16)
           + _nbytes((tm, D_FF), BF16) + 8 * _nbytes((tm + 2 * HALO_ROWS, FFN_CHUNK), F32))
    return pl.pallas_call(
        functools.partial(_ffn_body, tm=tm, n_tiles=n_tiles),
        grid=(bsz, n_tiles),
        in_specs=[prev, _row_spec(tm), nxt, _vec_spec(), _mod_spec(layer, 3, ctx), _mod_spec(layer, 4, ctx),
                  _layer_spec(wup.shape, layer), _const_spec(w_dw.shape), _const_spec(b_dw.shape),
                  _layer_spec(wd.shape, layer), _vec_spec(), _mod_spec(layer, 5, ctx)],
        out_specs=_row_spec(tm),
        out_shape=jax.ShapeDtypeStruct((bsz, t, D_MODEL), F32),
        compiler_params=_params(est),
        name="conv_ffn",
    )(x, x, x, gain, mods, mods, wup, w_dw, b_dw, wd, post_g, mods)


def _proj(x, gain, mods, layer, ctx, w_bf, col0, n_out):
    bsz, t, _ = x.shape
    tm = _tile_rows(t, PROJ_ROWS)
    est = 2 * _nbytes((tm, D_MODEL), F32) + _nbytes(w_bf.shape, BF16) + (2 * n_out + 4) * _nbytes((tm, D_MODEL), F32)
    return pl.pallas_call(
        functools.partial(_proj_body, col0=col0),
        grid=(bsz, t // tm),
        in_specs=[_row_spec(tm), _vec_spec(), _mod_spec(layer, 0, ctx), _mod_spec(layer, 1, ctx),
                  _const_spec(w_bf.shape)],
        out_specs=[_row_spec(tm)] * n_out,
        out_shape=[jax.ShapeDtypeStruct((bsz, t, D_MODEL), BF16)] * n_out,
        compiler_params=_params(est),
        name="norm_proj",
    )(x, gain, mods, mods, w_bf)


def _out_proj(a, x, w_bf, post_g, mods, layer):
    bsz, t, _ = x.shape
    tm = _tile_rows(t, PROJ_ROWS)
    est = 6 * _nbytes((tm, D_MODEL), F32) + _nbytes(w_bf.shape, BF16)
    return pl.pallas_call(
        _out_proj_body,
        grid=(bsz, t // tm),
        in_specs=[_row_spec(tm), _row_spec(tm), _const_spec(w_bf.shape), _vec_spec(), _mod_spec(layer, 2, None)],
        out_specs=_row_spec(tm),
        out_shape=jax.ShapeDtypeStruct((bsz, t, D_MODEL), F32),
        compiler_params=_params(est),
        name="out_proj",
    )(a, x, w_bf, post_g, mods)


def _attention(q, k, v, kc, vc, bias):
    bsz, t, _ = q.shape
    rows = t // GRID_W
    c_len = kc.shape[1]
    rps = ATTN_ROWS_PER_STEP
    assert rows % rps == 0
    tq = rps * GRID_W
    est = (4 * _nbytes((t, D_MODEL), BF16) + 4 * _nbytes((c_len, D_MODEL), BF16)
           + _nbytes(bias.shape, F32) + 4 * _nbytes((tq, D_MODEL), BF16))
    return pl.pallas_call(
        functools.partial(_attn_body, rows=rows, rows_per_step=rps),
        grid=(bsz, rows // rps),
        in_specs=[
            pl.BlockSpec((None, tq, D_MODEL), lambda b, r: (b, r, 0)),
            pl.BlockSpec((None, t, D_MODEL), lambda b, r: (b, 0, 0)),
            pl.BlockSpec((None, t, D_MODEL), lambda b, r: (b, 0, 0)),
            pl.BlockSpec((None, c_len, D_MODEL), lambda b, r: (b, 0, 0)),
            pl.BlockSpec((None, c_len, D_MODEL), lambda b, r: (b, 0, 0)),
            _const_spec(bias.shape),
        ],
        out_specs=pl.BlockSpec((None, tq, D_MODEL), lambda b, r: (b, r, 0)),
        out_shape=jax.ShapeDtypeStruct((bsz, t, D_MODEL), BF16),
        compiler_params=_params(est),
        name="na_attention",
    )(q, k, v, kc, vc, bias)


def _attn_bias_table(rpb):
    n_pairs = NA_HEADS // 2
    n_s = 2 * NA_KH - 2
    rp = jnp.pad(rpb.astype(F32), ((0, 0), (0, 0), (0, GRID_W - (2 * NA_KW - 1))))
    src = jnp.concatenate([rp[:, :n_s], rp[:, 1:]], axis=-1)
    src = jnp.transpose(src.reshape(n_pairs, 2, n_s, ATTN_PAIR), (0, 2, 1, 3))[:, :, :, None, :]
    return pl.pallas_call(
        _bias_body,
        grid=(n_pairs,),
        in_specs=[pl.BlockSpec((None, n_s, 2, 1, ATTN_PAIR), lambda p: (p, 0, 0, 0, 0))],
        out_specs=pl.BlockSpec((None, n_s, 2 * GRID_W, ATTN_PAIR), lambda p: (p, 0, 0, 0)),
        out_shape=jax.ShapeDtypeStruct((n_pairs, n_s, 2 * GRID_W, ATTN_PAIR), F32),
        name="na_bias",
    )(src)


def kernel(x, c, ctx, c_ctx, mod_w, mod_b, mix_pre_g, mix_post_g, ffn_pre_g, ffn_post_g, cv_w_pw1, cv_b_pw1, cv_w_dw, cv_b_dw, cv_ln_g, cv_ln_b, cv_w_pw2, cv_b_pw2, na_w_qkv, na_w_o, na_rpb, ffn_w_up, ffn_w_dw, ffn_b_dw, ffn_w_down):
    bsz = x.shape[0]
    assert bsz + 1 <= MOD_ROWS and x.shape[2] == D_MODEL and mod_w.shape[0] == DEPTH
    ctx_row = bsz
    cs = jnp.concatenate([c, c_ctx[None, :], jnp.zeros((MOD_ROWS - bsz - 1, D_MODEL), F32)], axis=0)
    mods = _ada_mod(cs, mod_w, mod_b)
    wup_all = ffn_w_up.astype(BF16)
    wd_all = ffn_w_down

    h_ctx = ctx
    for i in range(DEPTH):
        last = i == DEPTH - 1
        j = i // 2
        use_na = (i % 2) == 1
        pre_g = mix_pre_g[i][None, :]
        post_g = mix_post_g[i][None, :]
        ffn = (wup_all, jnp.broadcast_to(ffn_w_dw[i][:, None, :], (FFN_CONV_WIDTH, V7X_SUBLANES, D_FF)),
               jnp.broadcast_to(ffn_b_dw[i][None, :], (V7X_SUBLANES, D_FF)), wd_all)
        if use_na:
            scale = NA_HEAD_DIM ** -0.5
            col_scale = jnp.concatenate([jnp.full((D_MODEL,), scale, F32), jnp.ones((2 * D_MODEL,), F32)])
            w_qkv = (na_w_qkv[j] * col_scale[None, :]).astype(BF16)
            q, k, v = _proj(x, pre_g, mods, i, None, w_qkv, 0, 3)
            kc, vc = _proj(h_ctx, pre_g, mods, i, ctx_row, w_qkv, D_MODEL, 2)
            attn = _attention(q, k, v, kc, vc, _attn_bias_table(na_rpb[j]))
            x = _out_proj(attn, x, na_w_o[j].astype(BF16), post_g, mods, i)
            assert last, "context queries (an attention layer that is not the last) are not implemented"
        else:
            w1 = cv_w_pw1[j].astype(BF16)
            b1 = cv_b_pw1[j][None, :]
            w_dw = jnp.broadcast_to(cv_w_dw[j][:, None, :], (CONV_WIDTH, V7X_SUBLANES, D_MODEL))
            b_dw = jnp.broadcast_to(cv_b_dw[j][None, :], (V7X_SUBLANES, D_MODEL))
            w2 = cv_w_pw2[j].astype(BF16)
            cv = (w_dw, b_dw, cv_ln_g[j][None, :], cv_ln_b[j][None, :], w2, cv_b_pw2[j][None, :], post_g)
            u = _pw1_glu(x, pre_g, mods, i, None, w1, b1)
            x = _conv_mix(u, x, *cv, mods, i, None)
            if not last:
                u_ctx = _pw1_glu(h_ctx, pre_g, mods, i, ctx_row, w1, b1)
                h_ctx = _conv_mix(u_ctx, h_ctx, *cv, mods, i, ctx_row)
        x = _ffn(x, ffn_pre_g[i][None, :], ffn_post_g[i][None, :], mods, i, None, *ffn)
        if not last:
            h_ctx = _ffn(h_ctx, ffn_pre_g[i][None, :], ffn_post_g[i][None, :], mods, i, ctx_row, *ffn)
    return x
```

```python
import functools
import math

import jax
import jax.numpy as jnp
from jax import lax
from jax.experimental import pallas as pl
from jax.experimental.pallas import tpu as pltpu

D_MODEL = 1024
DEPTH = 2
GRID_W = 64
CONV_WIDTH = 31
NA_HEADS = 16
NA_HEAD_DIM = D_MODEL // NA_HEADS
NA_KH = 8
NA_KW = 16
D_FF = 2816
FFN_CONV_WIDTH = 3
N_MOD = 6
RMS_EPS = 1e-6
LN_EPS = 1e-5
NEG_INF = -1e30

V7X_LANES = 128
V7X_SUBLANES = 8
V7X_VMEM_BYTES = 64 * 1024 * 1024

HALO_ROWS = 2 * V7X_SUBLANES
MOD_ROWS = 16
FFN_CHUNK = 256
FFN_ROWS = 512
FFN_TAIL_BLOCKS = 2
CONV_ROWS = 128
PROJ_ROWS = 1024
ATTN_PAIR = 2 * NA_HEAD_DIM
ATTN_ROWS_PER_STEP = 4
ATTN_LOOKAHEAD = 3

F32 = jnp.float32
BF16 = jnp.bfloat16


def _vmem_limit(nbytes):
    return int(min(V7X_VMEM_BYTES - 6 * 1024 * 1024, max(2 * nbytes, 24 * 1024 * 1024)))


def _nbytes(shape, dtype):
    return math.prod(shape) * jnp.dtype(dtype).itemsize


def _modnorm(xf, gain, shift, scale):
    ms = jnp.mean(xf * xf, axis=-1, keepdims=True)
    return (xf * lax.rsqrt(ms + RMS_EPS)) * (gain * (1.0 + scale)) + shift


def _post_residual(xf, y, post_g, gate):
    ms = jnp.mean(y * y, axis=-1, keepdims=True)
    return xf + gate * ((y * lax.rsqrt(ms + RMS_EPS)) * post_g)


def _gelu_tanh(x):
    c = math.sqrt(2.0 / math.pi)
    return x * (0.5 * (1.0 + jnp.tanh(c * (x + 0.044715 * (x * x * x)))))


def _dwconv_rows(xe3, taps, pad, n, sub):
    out = None
    for r in range(V7X_SUBLANES):
        offs = [d for d in range(-pad, pad + 1) if d % V7X_SUBLANES == r]
        if not offs:
            continue
        rr = xe3 if r == 0 else pltpu.roll(xe3, V7X_SUBLANES - r, axis=1)
        m = n if r == 0 else n + 1
        part = None
        for d in offs:
            a = d // V7X_SUBLANES
            term = taps[d + pad] * rr[2 + a: 2 + a + m]
            part = term if part is None else part + term
        if r != 0:
            part = jnp.where(sub < V7X_SUBLANES - r, part[0:n], part[1:n + 1])
        out = part if out is None else out + part
    return out


def _ada_body(cs_ref, w_ref, b_ref, o_ref):
    s = cs_ref[...]
    s = s * jax.nn.sigmoid(s)
    m = jnp.dot(s.astype(BF16), w_ref[...].astype(BF16), preferred_element_type=F32) + b_ref[...]
    for row in range(MOD_ROWS):
        o_ref[row] = m[row:row + 1, :]


def _pw1_glu_body(x_ref, gain_ref, shift_ref, scale_ref, w_ref, b_ref, o_ref, *, tm):
    gain, shift, scale = gain_ref[...], shift_ref[...], scale_ref[...]
    rb = tm // 2
    hs = [_modnorm(x_ref[q * rb:(q + 1) * rb], gain, shift, scale).astype(BF16) for q in range(2)]
    for q, h in enumerate(hs):
        a = jnp.dot(h, w_ref[:, :D_MODEL], preferred_element_type=F32) + b_ref[:, :D_MODEL]
        g = jnp.dot(h, w_ref[:, D_MODEL:], preferred_element_type=F32) + b_ref[:, D_MODEL:]
        o_ref[q * rb:(q + 1) * rb] = a * jax.nn.sigmoid(g)


def _conv_mix_body(up_ref, u_ref, un_ref, x_ref, wdw_ref, bdw_ref, lng_ref, lnb_ref, w2_ref, b2_ref,
                   postg_ref, gate_ref, o_ref, ue_scr, cv_scr, *, tm, n_tiles):
    i = pl.program_id(1)
    for s in range(D_MODEL // V7X_LANES):
        lanes = slice(s * V7X_LANES, (s + 1) * V7X_LANES)
        ue_scr[s, 0:HALO_ROWS] = jnp.where(i > 0, up_ref[:, lanes], 0.0)
        ue_scr[s, HALO_ROWS:HALO_ROWS + tm] = u_ref[:, lanes]
        ue_scr[s, HALO_ROWS + tm:] = jnp.where(i < n_tiles - 1, un_ref[:, lanes], 0.0)

    first = HALO_ROWS - CONV_WIDTH // 2

    def conv_step(ci, carry):
        base = pl.multiple_of(ci * CONV_ROWS, CONV_ROWS)
        for s in range(D_MODEL // V7X_LANES):
            lanes = slice(s * V7X_LANES, (s + 1) * V7X_LANES)
            acc = None
            for k in range(CONV_WIDTH):
                shifted = ue_scr[s, pl.ds(base + first + k, CONV_ROWS, stride=1), :]
                term = wdw_ref[k, 0:1, lanes] * shifted
                acc = term if acc is None else acc + term
            cv_scr[pl.ds(base, CONV_ROWS), lanes] = acc + bdw_ref[0:1, lanes]
        return carry

    lax.fori_loop(0, tm // CONV_ROWS, conv_step, 0)

    u = cv_scr[...]
    uc = u - jnp.mean(u, axis=-1, keepdims=True)
    ln = (uc * lax.rsqrt(jnp.mean(uc * uc, axis=-1, keepdims=True) + LN_EPS)) * lng_ref[...] + lnb_ref[...]
    act = (ln * jax.nn.sigmoid(ln)).astype(BF16)
    y = jnp.dot(act, w2_ref[...], preferred_element_type=F32) + b2_ref[...]
    o_ref[...] = _post_residual(x_ref[...], y, postg_ref[...], gate_ref[...])


def _ffn_body(xp_ref, x_ref, xn_ref, gain_ref, shift_ref, scale_ref, wup_ref, wdw_ref, bdw_ref, wd_ref,
              postg_ref, gate_ref, o_ref, *, tm, n_tiles):
    i = pl.program_id(1)
    gain, shift, scale = gain_ref[...], shift_ref[...], scale_ref[...]
    x = x_ref[...]
    hp = jnp.where(i > 0, _modnorm(xp_ref[...], gain, shift, scale), 0.0).astype(BF16)
    hn = jnp.where(i < n_tiles - 1, _modnorm(xn_ref[...], gain, shift, scale), 0.0).astype(BF16)
    half = tm // 2
    h_lo = _modnorm(x[:half], gain, shift, scale).astype(BF16)
    h_hi = _modnorm(x[half:], gain, shift, scale).astype(BF16)
    h = jnp.concatenate([h_lo, h_hi], axis=0)
    he = jnp.concatenate([hp, h, hn], axis=0)

    n = tm // V7X_SUBLANES
    sub = lax.broadcasted_iota(jnp.int32, (n, V7X_SUBLANES, FFN_CHUNK), 1)
    n_chunks = D_FF // FFN_CHUNK

    def up_proj(j):
        cols = slice(j * FFN_CHUNK, (j + 1) * FFN_CHUNK)
        vcols = slice(D_FF + j * FFN_CHUNK, D_FF + (j + 1) * FFN_CHUNK)
        if j == 0:
            lo_rows = HALO_ROWS + half
            ge_lo = jnp.dot(he[:lo_rows], wup_ref[:, cols], preferred_element_type=F32)
            val_lo = jnp.dot(h_lo, wup_ref[:, vcols], preferred_element_type=F32)
            ge_hi = jnp.dot(he[lo_rows:], wup_ref[:, cols], preferred_element_type=F32)
            val_hi = jnp.dot(h_hi, wup_ref[:, vcols], preferred_element_type=F32)
            return jnp.concatenate([ge_lo, ge_hi], axis=0), jnp.concatenate([val_lo, val_hi], axis=0)
        ge = jnp.dot(he, wup_ref[:, cols], preferred_element_type=F32)
        val = jnp.dot(h, wup_ref[:, vcols], preferred_element_type=F32)
        return ge, val

    acts = []
    nxt = up_proj(0)
    for j in range(n_chunks):
        ge, val = nxt
        if j + 1 < n_chunks:
            nxt = up_proj(j + 1)
        cols = slice(j * FFN_CHUNK, (j + 1) * FFN_CHUNK)
        ge3 = ge.reshape(n + 4, V7X_SUBLANES, FFN_CHUNK)
        taps = [wdw_ref[k, :, cols] for k in range(FFN_CONV_WIDTH)]
        gc = _dwconv_rows(ge3, taps, FFN_CONV_WIDTH // 2, n, sub) + bdw_ref[:, cols]
        acts.append((_gelu_tanh(gc) * val.reshape(n, V7X_SUBLANES, FFN_CHUNK)).reshape(tm, FFN_CHUNK).astype(BF16))
    act = jnp.concatenate(acts, axis=1)

    rb = tm // FFN_TAIL_BLOCKS
    wd = wd_ref[...].astype(BF16)
    for q in range(FFN_TAIL_BLOCKS):
        rows = slice(q * rb, (q + 1) * rb)
        y = jnp.dot(act[rows], wd, preferred_element_type=F32)
        o_ref[rows] = _post_residual(x[rows], y, postg_ref[...], gate_ref[...])


def _proj_body(x_ref, gain_ref, shift_ref, scale_ref, w_ref, *o_refs, col0):
    h = _modnorm(x_ref[...], gain_ref[...], shift_ref[...], scale_ref[...]).astype(BF16)
    for t, o_ref in enumerate(o_refs):
        cols = slice(col0 + t * D_MODEL, col0 + (t + 1) * D_MODEL)
        o_ref[...] = jnp.dot(h, w_ref[:, cols], preferred_element_type=F32).astype(o_ref.dtype)


def _bias_body(src_ref, o_ref):
    qc = lax.broadcasted_iota(jnp.int32, (GRID_W, ATTN_PAIR), 0)
    kc = lax.broadcasted_iota(jnp.int32, (GRID_W, ATTN_PAIR), 1) & (GRID_W - 1)
    c0 = jnp.clip(qc - NA_KW // 2, 0, GRID_W - NA_KW)
    in_win = (kc >= c0) & (kc < c0 + NA_KW)
    base_shift = V7X_LANES - (NA_KW - 1)
    for s in range(2 * NA_KH - 2):
        for e in range(2):
            src = jnp.broadcast_to(src_ref[s, e], (GRID_W, ATTN_PAIR))
            tile = pltpu.roll(src, base_shift, 1, stride=1, stride_axis=0)
            o_ref[s, e * GRID_W:(e + 1) * GRID_W, :] = jnp.where(in_win, tile, NEG_INF)


def _out_proj_body(a_ref, x_ref, w_ref, postg_ref, gate_ref, o_ref):
    y = jnp.dot(a_ref[...], w_ref[...].astype(BF16), preferred_element_type=F32)
    o_ref[...] = _post_residual(x_ref[...], y, postg_ref[...], gate_ref[...])


def _attn_body(q_ref, k_ref, v_ref, kc_ref, vc_ref, bias_ref, o_ref, *, rows, rows_per_step):
    n_loc = NA_KH * GRID_W
    lo = lax.broadcasted_iota(jnp.int32, (GRID_W, ATTN_PAIR), 1) < NA_HEAD_DIM
    nt = (((1,), (1,)), ((), ()))

    def scores(p, rr):
        lanes = slice(p * ATTN_PAIR, (p + 1) * ATTN_PAIR)
        r = pl.program_id(1) * rows_per_step + rr
        r0 = jnp.clip(r - NA_KH // 2, 0, rows - NA_KH)
        row_off = r0 - r + (NA_KH - 1)
        start = pl.multiple_of(r0 * GRID_W, GRID_W)
        qp = q_ref[rr * GRID_W:(rr + 1) * GRID_W, lanes]
        zero = jnp.zeros_like(qp)
        qs = jnp.concatenate([jnp.where(lo, qp, zero), jnp.where(lo, zero, qp)], axis=0)
        bias = jnp.concatenate([bias_ref[p, row_off + 2 * jj] for jj in range(NA_KH // 2)], axis=1)
        k_all = jnp.concatenate([k_ref[pl.ds(start, n_loc), lanes], kc_ref[:, lanes]], axis=0)
        s = lax.dot_general(qs, k_all, nt, preferred_element_type=F32)
        return s[:, :n_loc] + bias, s[:, n_loc:], start

    units = [(p, rr) for p in range(NA_HEADS // 2) for rr in range(rows_per_step)]
    pending = [scores(*units[i]) for i in range(ATTN_LOOKAHEAD)]
    for idx, (p, rr) in enumerate(units):
        s_loc, s_ctx, start = pending.pop(0)
        if idx + ATTN_LOOKAHEAD < len(units):
            pending.append(scores(*units[idx + ATTN_LOOKAHEAD]))
        lanes = slice(p * ATTN_PAIR, (p + 1) * ATTN_PAIR)
        m = jnp.maximum(jnp.max(s_loc, axis=-1, keepdims=True), jnp.max(s_ctx, axis=-1, keepdims=True))
        probs = jnp.concatenate([jnp.exp(s_loc - m).astype(BF16), jnp.exp(s_ctx - m).astype(BF16)], axis=1)
        v_all = jnp.concatenate([v_ref[pl.ds(start, n_loc), lanes], vc_ref[:, lanes]], axis=0)
        v_ext = jnp.concatenate([v_all, jnp.ones_like(v_all)], axis=1)
        o_ext = jnp.dot(probs, v_ext, preferred_element_type=F32)
        o = o_ext[:, :ATTN_PAIR] / o_ext[:, ATTN_PAIR:ATTN_PAIR + 1]
        o_ref[rr * GRID_W:(rr + 1) * GRID_W, lanes] = jnp.where(lo, o[:GRID_W], o[GRID_W:]).astype(o_ref.dtype)


def _tile_rows(t, pref=512):
    return pref if t % pref == 0 else t


def _row_spec(tm, width=D_MODEL):
    return pl.BlockSpec((None, tm, width), lambda b, i: (b, i, 0))


def _halo_specs(tm, t):
    per = tm // HALO_ROWS
    last = t // HALO_ROWS - 1
    prev = pl.BlockSpec((None, HALO_ROWS, D_MODEL), lambda b, i: (b, jnp.maximum(i * per - 1, 0), 0))
    nxt = pl.BlockSpec((None, HALO_ROWS, D_MODEL), lambda b, i: (b, jnp.minimum((i + 1) * per, last), 0))
    return prev, nxt


def _const_spec(shape):
    nd = len(shape)
    return pl.BlockSpec(shape, lambda b, i: (0,) * nd, pipeline_mode=pl.Buffered(1))


def _vec_spec():
    return pl.BlockSpec((1, D_MODEL), lambda b, i: (0, 0))


def _mod_spec(layer, comp, ctx):
    if ctx is None:
        return pl.BlockSpec((None, None, None, 1, D_MODEL), lambda b, i: (layer, comp, b, 0, 0))
    return pl.BlockSpec((None, None, None, 1, D_MODEL), lambda b, i: (layer, comp, ctx, 0, 0))


def _params(nbytes):
    return pltpu.CompilerParams(dimension_semantics=("arbitrary", "arbitrary"), vmem_limit_bytes=_vmem_limit(nbytes))


def _ada_mod(cs, mod_w, mod_b):
    depth = mod_w.shape[0]
    blk = _nbytes((D_MODEL, D_MODEL), F32)
    return pl.pallas_call(
        _ada_body,
        grid=(depth, N_MOD),
        in_specs=[
            pl.BlockSpec((MOD_ROWS, D_MODEL), lambda l, k: (0, 0)),
            pl.BlockSpec((None, D_MODEL, D_MODEL), lambda l, k: (l, 0, k)),
            pl.BlockSpec((None, 1, D_MODEL), lambda l, k: (l, 0, k)),
        ],
        out_specs=pl.BlockSpec((None, None, MOD_ROWS, 1, D_MODEL), lambda l, k: (l, k, 0, 0, 0)),
        out_shape=jax.ShapeDtypeStruct((depth, N_MOD, MOD_ROWS, 1, D_MODEL), F32),
        compiler_params=pltpu.CompilerParams(dimension_semantics=("arbitrary", "arbitrary"),
                                             vmem_limit_bytes=_vmem_limit(4 * blk)),
        name="ada_mod",
    )(cs, mod_w, mod_b.reshape(depth, 1, N_MOD * D_MODEL))


def _pw1_glu(x, gain, mods, layer, ctx, w_bf, b):
    bsz, t, _ = x.shape
    tm = _tile_rows(t, PROJ_ROWS)
    est = 4 * _nbytes((tm, D_MODEL), F32) + _nbytes(w_bf.shape, BF16) + 3 * _nbytes((tm, 2 * D_MODEL), F32)
    return pl.pallas_call(
        functools.partial(_pw1_glu_body, tm=tm),
        grid=(bsz, t // tm),
        in_specs=[_row_spec(tm), _vec_spec(), _mod_spec(layer, 0, ctx), _mod_spec(layer, 1, ctx),
                  _const_spec(w_bf.shape), _const_spec(b.shape)],
        out_specs=_row_spec(tm),
        out_shape=jax.ShapeDtypeStruct((bsz, t, D_MODEL), F32),
        compiler_params=_params(est),
        name="pw1_glu",
    )(x, gain, mods, mods, w_bf, b)


def _conv_mix(u, x, w_dw, b_dw, ln_g, ln_b, w2_bf, b2, post_g, mods, layer, ctx):
    bsz, t, _ = x.shape
    tm = _tile_rows(t)
    n_tiles = t // tm
    prev, nxt = _halo_specs(tm, t)
    est = (8 * _nbytes((tm, D_MODEL), F32) + _nbytes(w2_bf.shape, BF16) + _nbytes(w_dw.shape, F32)
           + 4 * _nbytes((tm, D_MODEL), F32))
    return pl.pallas_call(
        functools.partial(_conv_mix_body, tm=tm, n_tiles=n_tiles),
        grid=(bsz, n_tiles),
        in_specs=[prev, _row_spec(tm), nxt, _row_spec(tm),
                  _const_spec(w_dw.shape), _const_spec(b_dw.shape), _vec_spec(), _vec_spec(),
                  _const_spec(w2_bf.shape), _vec_spec(), _vec_spec(), _mod_spec(layer, 2, ctx)],
        out_specs=_row_spec(tm),
        out_shape=jax.ShapeDtypeStruct((bsz, t, D_MODEL), F32),
        scratch_shapes=[pltpu.VMEM((D_MODEL // V7X_LANES, tm + 2 * HALO_ROWS, V7X_LANES), F32),
                        pltpu.VMEM((tm, D_MODEL), F32)],
        compiler_params=_params(est),
        name="conv_mix",
    )(u, u, u, x, w_dw, b_dw, ln_g, ln_b, w2_bf, b2, post_g, mods)


def _layer_spec(stacked_shape, layer):
    nd = len(stacked_shape) - 1
    return pl.BlockSpec((None,) + tuple(stacked_shape[1:]), lambda b, i: (layer,) + (0,) * nd,
                        pipeline_mode=pl.Buffered(1))


def _ffn(x, gain, post_g, mods, layer, ctx, wup, w_dw, b_dw, wd):
    bsz, t, _ = x.shape
    tm = _tile_rows(t, FFN_ROWS)
    n_tiles = t // tm
    prev, nxt = _halo_specs(tm, t)
    est = (8 * _nbytes((tm, D_MODEL), F32) + _nbytes(wup.shape[1:], BF16) + _nbytes(wd.shape[1:], F32) + _nbytes(wd.shape[1:], BF16)
           + _nbytes((tm, D_FF), BF16) + 8 * _nbytes((tm + 2 * HALO_ROWS, FFN_CHUNK), F32))
    return pl.pallas_call(
        functools.partial(_ffn_body, tm=tm, n_tiles=n_tiles),
        grid=(bsz, n_tiles),
        in_specs=[prev, _row_spec(tm), nxt, _vec_spec(), _mod_spec(layer, 3, ctx), _mod_spec(layer, 4, ctx),
                  _layer_spec(wup.shape, layer), _const_spec(w_dw.shape), _const_spec(b_dw.shape),
                  _layer_spec(wd.shape, layer), _vec_spec(), _mod_spec(layer, 5, ctx)],
        out_specs=_row_spec(tm),
        out_shape=jax.ShapeDtypeStruct((bsz, t, D_MODEL), F32),
        compiler_params=_params(est),
        name="conv_ffn",
    )(x, x, x, gain, mods, mods, wup, w_dw, b_dw, wd, post_g, mods)


def _proj(x, gain, mods, layer, ctx, w_bf, col0, n_out):
    bsz, t, _ = x.shape
    tm = _tile_rows(t, PROJ_ROWS)
    est = 2 * _nbytes((tm, D_MODEL), F32) + _nbytes(w_bf.shape, BF16) + (2 * n_out + 4) * _nbytes((tm, D_MODEL), F32)
    return pl.pallas_call(
        functools.partial(_proj_body, col0=col0),
        grid=(bsz, t // tm),
        in_specs=[_row_spec(tm), _vec_spec(), _mod_spec(layer, 0, ctx), _mod_spec(layer, 1, ctx),
                  _const_spec(w_bf.shape)],
        out_specs=[_row_spec(tm)] * n_out,
        out_shape=[jax.ShapeDtypeStruct((bsz, t, D_MODEL), BF16)] * n_out,
        compiler_params=_params(est),
        name="norm_proj",
    )(x, gain, mods, mods, w_bf)


def _out_proj(a, x, w, post_g, mods, layer):
    bsz, t, _ = x.shape
    tm = _tile_rows(t, PROJ_ROWS)
    est = 6 * _nbytes((tm, D_MODEL), F32) + _nbytes(w.shape, F32) + _nbytes(w.shape, BF16)
    return pl.pallas_call(
        _out_proj_body,
        grid=(bsz, t // tm),
        in_specs=[_row_spec(tm), _row_spec(tm), _const_spec(w.shape), _vec_spec(), _mod_spec(layer, 2, None)],
        out_specs=_row_spec(tm),
        out_shape=jax.ShapeDtypeStruct((bsz, t, D_MODEL), F32),
        compiler_params=_params(est),
        name="out_proj",
    )(a, x, w, post_g, mods)


def _attention(q, k, v, kc, vc, bias):
    bsz, t, _ = q.shape
    rows = t // GRID_W
    c_len = kc.shape[1]
    rps = ATTN_ROWS_PER_STEP
    assert rows % rps == 0
    tq = rps * GRID_W
    est = (4 * _nbytes((t, D_MODEL), BF16) + 4 * _nbytes((c_len, D_MODEL), BF16)
           + _nbytes(bias.shape, F32) + 4 * _nbytes((tq, D_MODEL), BF16))
    return pl.pallas_call(
        functools.partial(_attn_body, rows=rows, rows_per_step=rps),
        grid=(bsz, rows // rps),
        in_specs=[
            pl.BlockSpec((None, tq, D_MODEL), lambda b, r: (b, r, 0)),
            pl.BlockSpec((None, t, D_MODEL), lambda b, r: (b, 0, 0)),
            pl.BlockSpec((None, t, D_MODEL), lambda b, r: (b, 0, 0)),
            pl.BlockSpec((None, c_len, D_MODEL), lambda b, r: (b, 0, 0)),
            pl.BlockSpec((None, c_len, D_MODEL), lambda b, r: (b, 0, 0)),
            _const_spec(bias.shape),
        ],
        out_specs=pl.BlockSpec((None, tq, D_MODEL), lambda b, r: (b, r, 0)),
        out_shape=jax.ShapeDtypeStruct((bsz, t, D_MODEL), BF16),
        compiler_params=_params(est),
        name="na_attention",
    )(q, k, v, kc, vc, bias)


def _attn_bias_table(rpb):
    n_pairs = NA_HEADS // 2
    n_s = 2 * NA_KH - 2
    rp = jnp.pad(rpb.astype(F32), ((0, 0), (0, 0), (0, GRID_W - (2 * NA_KW - 1))))
    src = jnp.concatenate([rp[:, :n_s], rp[:, 1:]], axis=-1)
    src = jnp.transpose(src.reshape(n_pairs, 2, n_s, ATTN_PAIR), (0, 2, 1, 3))[:, :, :, None, :]
    return pl.pallas_call(
        _bias_body,
        grid=(n_pairs,),
        in_specs=[pl.BlockSpec((None, n_s, 2, 1, ATTN_PAIR), lambda p: (p, 0, 0, 0, 0))],
        out_specs=pl.BlockSpec((None, n_s, 2 * GRID_W, ATTN_PAIR), lambda p: (p, 0, 0, 0)),
        out_shape=jax.ShapeDtypeStruct((n_pairs, n_s, 2 * GRID_W, ATTN_PAIR), F32),
        name="na_bias",
    )(src)


def kernel(x, c, ctx, c_ctx, mod_w, mod_b, mix_pre_g, mix_post_g, ffn_pre_g, ffn_post_g, cv_w_pw1, cv_b_pw1, cv_w_dw, cv_b_dw, cv_ln_g, cv_ln_b, cv_w_pw2, cv_b_pw2, na_w_qkv, na_w_o, na_rpb, ffn_w_up, ffn_w_dw, ffn_b_dw, ffn_w_down):
    bsz = x.shape[0]
    assert bsz + 1 <= MOD_ROWS and x.shape[2] == D_MODEL and mod_w.shape[0] == DEPTH
    ctx_row = bsz
    cs = jnp.concatenate([c, c_ctx[None, :], jnp.zeros((MOD_ROWS - bsz - 1, D_MODEL), F32)], axis=0)
    mods = _ada_mod(cs, mod_w, mod_b)
    wup_all = ffn_w_up.astype(BF16)
    wd_all = ffn_w_down

    h_ctx = ctx
    for i in range(DEPTH):
        last = i == DEPTH - 1
        j = i // 2
        use_na = (i % 2) == 1
        pre_g = mix_pre_g[i][None, :]
        post_g = mix_post_g[i][None, :]
        ffn = (wup_all, jnp.broadcast_to(ffn_w_dw[i][:, None, :], (FFN_CONV_WIDTH, V7X_SUBLANES, D_FF)),
               jnp.broadcast_to(ffn_b_dw[i][None, :], (V7X_SUBLANES, D_FF)), wd_all)
        if use_na:
            scale = NA_HEAD_DIM ** -0.5
            col_scale = jnp.concatenate([jnp.full((D_MODEL,), scale, F32), jnp.ones((2 * D_MODEL,), F32)])
            w_qkv = (na_w_qkv[j] * col_scale[None, :]).astype(BF16)
            q, k, v = _proj(x, pre_g, mods, i, None, w_qkv, 0, 3)
            kc, vc = _proj(h_ctx, pre_g, mods, i, ctx_row, w_qkv, D_MODEL, 2)
            attn = _attention(q, k, v, kc, vc, _attn_bias_table(na_rpb[j]))
            x = _out_proj(attn, x, na_w_o[j], post_g, mods, i)
            assert last, "context queries (an attention layer that is not the last) are not implemented"
        else:
            w1 = cv_w_pw1[j].astype(BF16)
            b1 = cv_b_pw1[j][None, :]
            w_dw = jnp.broadcast_to(cv_w_dw[j][:, None, :], (CONV_WIDTH, V7X_SUBLANES, D_MODEL))
            b_dw = jnp.broadcast_to(cv_b_dw[j][None, :], (V7X_SUBLANES, D_MODEL))
            w2 = cv_w_pw2[j].astype(BF16)
            cv = (w_dw, b_dw, cv_ln_g[j][None, :], cv_ln_b[j][None, :], w2, cv_b_pw2[j][None, :], post_g)
            u = _pw1_glu(x, pre_g, mods, i, None, w1, b1)
            x = _conv_mix(u, x, *cv, mods, i, None)
            if not last:
                u_ctx = _pw1_glu(h_ctx, pre_g, mods, i, ctx_row, w1, b1)
                h_ctx = _conv_mix(u_ctx, h_ctx, *cv, mods, i, ctx_row)
        x = _ffn(x, ffn_pre_g[i][None, :], ffn_post_g[i][None, :], mods, i, None, *ffn)
        if not last:
            h_ctx = _ffn(h_ctx, ffn_pre_g[i][None, :], ffn_post_g[i][None, :], mods, i, ctx_row, *ffn)
    return x
```

```python
import functools
import math

import jax
import jax.numpy as jnp
from jax import lax
from jax.experimental import pallas as pl
from jax.experimental.pallas import tpu as pltpu

D_MODEL = 1024
DEPTH = 2
GRID_W = 64
CONV_WIDTH = 31
NA_HEADS = 16
NA_HEAD_DIM = D_MODEL // NA_HEADS
NA_KH = 8
NA_KW = 16
D_FF = 2816
FFN_CONV_WIDTH = 3
N_MOD = 6
RMS_EPS = 1e-6
LN_EPS = 1e-5
NEG_INF = -1e30

V7X_LANES = 128
V7X_SUBLANES = 8
V7X_VMEM_BYTES = 64 * 1024 * 1024

HALO_ROWS = 2 * V7X_SUBLANES
MOD_ROWS = 16
FFN_CHUNK = 256
FFN_ROWS = 512
FFN_TAIL_BLOCKS = 2
CONV_ROWS = 128
PROJ_ROWS = 1024
ATTN_PAIR = 2 * NA_HEAD_DIM
ATTN_ROWS_PER_STEP = 4
ATTN_LOOKAHEAD = 3

F32 = jnp.float32
BF16 = jnp.bfloat16


def _vmem_limit(nbytes):
    return int(min(V7X_VMEM_BYTES - 6 * 1024 * 1024, max(2 * nbytes, 24 * 1024 * 1024)))


def _nbytes(shape, dtype):
    return math.prod(shape) * jnp.dtype(dtype).itemsize


def _modnorm(xf, gain, shift, scale):
    ms = jnp.mean(xf * xf, axis=-1, keepdims=True)
    return (xf * lax.rsqrt(ms + RMS_EPS)) * (gain * (1.0 + scale)) + shift


def _post_residual(xf, y, post_g, gate):
    ms = jnp.mean(y * y, axis=-1, keepdims=True)
    return xf + gate * ((y * lax.rsqrt(ms + RMS_EPS)) * post_g)


def _gelu_tanh(x):
    c = math.sqrt(2.0 / math.pi)
    return x * (0.5 * (1.0 + jnp.tanh(c * (x + 0.044715 * (x * x * x)))))


def _dwconv_rows(xe3, taps, pad, n, sub):
    out = None
    for r in range(V7X_SUBLANES):
        offs = [d for d in range(-pad, pad + 1) if d % V7X_SUBLANES == r]
        if not offs:
            continue
        rr = xe3 if r == 0 else pltpu.roll(xe3, V7X_SUBLANES - r, axis=1)
        m = n if r == 0 else n + 1
        part = None
        for d in offs:
            a = d // V7X_SUBLANES
            term = taps[d + pad] * rr[2 + a: 2 + a + m]
            part = term if part is None else part + term
        if r != 0:
            part = jnp.where(sub < V7X_SUBLANES - r, part[0:n], part[1:n + 1])
        out = part if out is None else out + part
    return out


def _ada_body(cs_ref, w_ref, b_ref, o_ref):
    s = cs_ref[...]
    s = s * jax.nn.sigmoid(s)
    m = jnp.dot(s.astype(BF16), w_ref[...].astype(BF16), preferred_element_type=F32) + b_ref[...]
    for row in range(MOD_ROWS):
        o_ref[row] = m[row:row + 1, :]


def _pw1_glu_body(x_ref, gain_ref, shift_ref, scale_ref, w_ref, b_ref, o_ref, *, tm):
    gain, shift, scale = gain_ref[...], shift_ref[...], scale_ref[...]
    rb = tm // 2
    hs = [_modnorm(x_ref[q * rb:(q + 1) * rb], gain, shift, scale).astype(BF16) for q in range(2)]
    for q, h in enumerate(hs):
        a = jnp.dot(h, w_ref[:, :D_MODEL], preferred_element_type=F32) + b_ref[:, :D_MODEL]
        g = jnp.dot(h, w_ref[:, D_MODEL:], preferred_element_type=F32) + b_ref[:, D_MODEL:]
        o_ref[q * rb:(q + 1) * rb] = a * jax.nn.sigmoid(g)


def _conv_mix_body(up_ref, u_ref, un_ref, x_ref, wdw_ref, bdw_ref, lng_ref, lnb_ref, w2_ref, b2_ref,
                   postg_ref, gate_ref, o_ref, ue_scr, cv_scr, *, tm, n_tiles):
    i = pl.program_id(1)
    for s in range(D_MODEL // V7X_LANES):
        lanes = slice(s * V7X_LANES, (s + 1) * V7X_LANES)
        ue_scr[s, 0:HALO_ROWS] = jnp.where(i > 0, up_ref[:, lanes], 0.0)
        ue_scr[s, HALO_ROWS:HALO_ROWS + tm] = u_ref[:, lanes]
        ue_scr[s, HALO_ROWS + tm:] = jnp.where(i < n_tiles - 1, un_ref[:, lanes], 0.0)

    first = HALO_ROWS - CONV_WIDTH // 2

    def conv_step(ci, carry):
        base = pl.multiple_of(ci * CONV_ROWS, CONV_ROWS)
        for s in range(D_MODEL // V7X_LANES):
            lanes = slice(s * V7X_LANES, (s + 1) * V7X_LANES)
            acc = None
            for k in range(CONV_WIDTH):
                shifted = ue_scr[s, pl.ds(base + first + k, CONV_ROWS, stride=1), :]
                term = wdw_ref[k, 0:1, lanes] * shifted
                acc = term if acc is None else acc + term
            cv_scr[pl.ds(base, CONV_ROWS), lanes] = acc + bdw_ref[0:1, lanes]
        return carry

    lax.fori_loop(0, tm // CONV_ROWS, conv_step, 0)

    u = cv_scr[...]
    uc = u - jnp.mean(u, axis=-1, keepdims=True)
    ln = (uc * lax.rsqrt(jnp.mean(uc * uc, axis=-1, keepdims=True) + LN_EPS)) * lng_ref[...] + lnb_ref[...]
    act = (ln * jax.nn.sigmoid(ln)).astype(BF16)
    y = jnp.dot(act, w2_ref[...], preferred_element_type=F32) + b2_ref[...]
    o_ref[...] = _post_residual(x_ref[...], y, postg_ref[...], gate_ref[...])


def _ffn_body(xp_ref, x_ref, xn_ref, gain_ref, shift_ref, scale_ref, wup_ref, wdw_ref, bdw_ref, wd_ref,
              postg_ref, gate_ref, o_ref, *, tm, n_tiles):
    i = pl.program_id(1)
    gain, shift, scale = gain_ref[...], shift_ref[...], scale_ref[...]
    x = x_ref[...]
    hp = jnp.where(i > 0, _modnorm(xp_ref[...], gain, shift, scale), 0.0).astype(BF16)
    hn = jnp.where(i < n_tiles - 1, _modnorm(xn_ref[...], gain, shift, scale), 0.0).astype(BF16)
    half = tm // 2
    h_lo = _modnorm(x[:half], gain, shift, scale).astype(BF16)
    h_hi = _modnorm(x[half:], gain, shift, scale).astype(BF16)
    h = jnp.concatenate([h_lo, h_hi], axis=0)
    he = jnp.concatenate([hp, h, hn], axis=0)

    n = tm // V7X_SUBLANES
    sub = lax.broadcasted_iota(jnp.int32, (n, V7X_SUBLANES, FFN_CHUNK), 1)
    n_chunks = D_FF // FFN_CHUNK

    def up_proj(j):
        cols = slice(j * FFN_CHUNK, (j + 1) * FFN_CHUNK)
        vcols = slice(D_FF + j * FFN_CHUNK, D_FF + (j + 1) * FFN_CHUNK)
        if j == 0:
            lo_rows = HALO_ROWS + half
            ge_lo = jnp.dot(he[:lo_rows], wup_ref[:, cols], preferred_element_type=F32)
            val_lo = jnp.dot(h_lo, wup_ref[:, vcols], preferred_element_type=F32)
            ge_hi = jnp.dot(he[lo_rows:], wup_ref[:, cols], preferred_element_type=F32)
            val_hi = jnp.dot(h_hi, wup_ref[:, vcols], preferred_element_type=F32)
            return jnp.concatenate([ge_lo, ge_hi], axis=0), jnp.concatenate([val_lo, val_hi], axis=0)
        ge = jnp.dot(he, wup_ref[:, cols], preferred_element_type=F32)
        val = jnp.dot(h, wup_ref[:, vcols], preferred_element_type=F32)
        return ge, val

    acts = []
    nxt = up_proj(0)
    for j in range(n_chunks):
        ge, val = nxt
        if j + 1 < n_chunks:
            nxt = up_proj(j + 1)
        cols = slice(j * FFN_CHUNK, (j + 1) * FFN_CHUNK)
        ge3 = ge.reshape(n + 4, V7X_SUBLANES, FFN_CHUNK)
        taps = [wdw_ref[k, :, cols] for k in range(FFN_CONV_WIDTH)]
        gc = _dwconv_rows(ge3, taps, FFN_CONV_WIDTH // 2, n, sub) + bdw_ref[:, cols]
        acts.append((_gelu_tanh(gc) * val.reshape(n, V7X_SUBLANES, FFN_CHUNK)).reshape(tm, FFN_CHUNK).astype(BF16))
    act = jnp.concatenate(acts, axis=1)

    rb = tm // FFN_TAIL_BLOCKS
    wd = wd_ref[...].astype(BF16)
    for q in range(FFN_TAIL_BLOCKS):
        rows = slice(q * rb, (q + 1) * rb)
        y = jnp.dot(act[rows], wd, preferred_element_type=F32)
        o_ref[rows] = _post_residual(x[rows], y, postg_ref[...], gate_ref[...])


def _proj_body(x_ref, gain_ref, shift_ref, scale_ref, w_ref, *o_refs, col0, out_scales):
    h = _modnorm(x_ref[...], gain_ref[...], shift_ref[...], scale_ref[...]).astype(BF16)
    for t, o_ref in enumerate(o_refs):
        cols = slice(col0 + t * D_MODEL, col0 + (t + 1) * D_MODEL)
        y = jnp.dot(h, w_ref[:, cols].astype(BF16), preferred_element_type=F32)
        if out_scales[t] != 1.0:
            y = y * out_scales[t]
        o_ref[...] = y.astype(o_ref.dtype)


def _bias_body(src_ref, o_ref):
    qc = lax.broadcasted_iota(jnp.int32, (GRID_W, ATTN_PAIR), 0)
    kc = lax.broadcasted_iota(jnp.int32, (GRID_W, ATTN_PAIR), 1) & (GRID_W - 1)
    c0 = jnp.clip(qc - NA_KW // 2, 0, GRID_W - NA_KW)
    in_win = (kc >= c0) & (kc < c0 + NA_KW)
    base_shift = V7X_LANES - (NA_KW - 1)
    for s in range(2 * NA_KH - 2):
        for e in range(2):
            src = jnp.broadcast_to(src_ref[s, e], (GRID_W, ATTN_PAIR))
            tile = pltpu.roll(src, base_shift, 1, stride=1, stride_axis=0)
            o_ref[s, e * GRID_W:(e + 1) * GRID_W, :] = jnp.where(in_win, tile, NEG_INF)


def _out_proj_body(a_ref, x_ref, w_ref, postg_ref, gate_ref, o_ref):
    y = jnp.dot(a_ref[...], w_ref[...].astype(BF16), preferred_element_type=F32)
    o_ref[...] = _post_residual(x_ref[...], y, postg_ref[...], gate_ref[...])


def _attn_body(q_ref, k_ref, v_ref, kc_ref, vc_ref, bias_ref, o_ref, *, rows, rows_per_step):
    n_loc = NA_KH * GRID_W
    lo = lax.broadcasted_iota(jnp.int32, (GRID_W, ATTN_PAIR), 1) < NA_HEAD_DIM
    nt = (((1,), (1,)), ((), ()))

    def scores(p, rr):
        lanes = slice(p * ATTN_PAIR, (p + 1) * ATTN_PAIR)
        r = pl.program_id(1) * rows_per_step + rr
        r0 = jnp.clip(r - NA_KH // 2, 0, rows - NA_KH)
        row_off = r0 - r + (NA_KH - 1)
        start = pl.multiple_of(r0 * GRID_W, GRID_W)
        qp = q_ref[rr * GRID_W:(rr + 1) * GRID_W, lanes]
        zero = jnp.zeros_like(qp)
        qs = jnp.concatenate([jnp.where(lo, qp, zero), jnp.where(lo, zero, qp)], axis=0)
        bias = jnp.concatenate([bias_ref[p, row_off + 2 * jj] for jj in range(NA_KH // 2)], axis=1)
        k_all = jnp.concatenate([k_ref[pl.ds(start, n_loc), lanes], kc_ref[:, lanes]], axis=0)
        s = lax.dot_general(qs, k_all, nt, preferred_element_type=F32)
        return s[:, :n_loc] + bias, s[:, n_loc:], start

    units = [(p, rr) for p in range(NA_HEADS // 2) for rr in range(rows_per_step)]
    pending = [scores(*units[i]) for i in range(ATTN_LOOKAHEAD)]
    for idx, (p, rr) in enumerate(units):
        s_loc, s_ctx, start = pending.pop(0)
        if idx + ATTN_LOOKAHEAD < len(units):
            pending.append(scores(*units[idx + ATTN_LOOKAHEAD]))
        lanes = slice(p * ATTN_PAIR, (p + 1) * ATTN_PAIR)
        m = jnp.maximum(jnp.max(s_loc, axis=-1, keepdims=True), jnp.max(s_ctx, axis=-1, keepdims=True))
        probs = jnp.concatenate([jnp.exp(s_loc - m).astype(BF16), jnp.exp(s_ctx - m).astype(BF16)], axis=1)
        v_all = jnp.concatenate([v_ref[pl.ds(start, n_loc), lanes], vc_ref[:, lanes]], axis=0)
        v_ext = jnp.concatenate([v_all, jnp.ones_like(v_all)], axis=1)
        o_ext = jnp.dot(probs, v_ext, preferred_element_type=F32)
        o = o_ext[:, :ATTN_PAIR] / o_ext[:, ATTN_PAIR:ATTN_PAIR + 1]
        o_ref[rr * GRID_W:(rr + 1) * GRID_W, lanes] = jnp.where(lo, o[:GRID_W], o[GRID_W:]).astype(o_ref.dtype)


def _tile_rows(t, pref=512):
    return pref if t % pref == 0 else t


def _row_spec(tm, width=D_MODEL):
    return pl.BlockSpec((None, tm, width), lambda b, i: (b, i, 0))


def _halo_specs(tm, t):
    per = tm // HALO_ROWS
    last = t // HALO_ROWS - 1
    prev = pl.BlockSpec((None, HALO_ROWS, D_MODEL), lambda b, i: (b, jnp.maximum(i * per - 1, 0), 0))
    nxt = pl.BlockSpec((None, HALO_ROWS, D_MODEL), lambda b, i: (b, jnp.minimum((i + 1) * per, last), 0))
    return prev, nxt


def _const_spec(shape):
    nd = len(shape)
    return pl.BlockSpec(shape, lambda b, i: (0,) * nd, pipeline_mode=pl.Buffered(1))


def _vec_spec():
    return pl.BlockSpec((1, D_MODEL), lambda b, i: (0, 0))


def _mod_spec(layer, comp, ctx):
    if ctx is None:
        return pl.BlockSpec((None, None, None, 1, D_MODEL), lambda b, i: (layer, comp, b, 0, 0))
    return pl.BlockSpec((None, None, None, 1, D_MODEL), lambda b, i: (layer, comp, ctx, 0, 0))


def _params(nbytes):
    return pltpu.CompilerParams(dimension_semantics=("arbitrary", "arbitrary"), vmem_limit_bytes=_vmem_limit(nbytes))


def _ada_mod(cs, mod_w, mod_b):
    depth = mod_w.shape[0]
    blk = _nbytes((D_MODEL, D_MODEL), F32)
    return pl.pallas_call(
        _ada_body,
        grid=(depth, N_MOD),
        in_specs=[
            pl.BlockSpec((MOD_ROWS, D_MODEL), lambda l, k: (0, 0)),
            pl.BlockSpec((None, D_MODEL, D_MODEL), lambda l, k: (l, 0, k)),
            pl.BlockSpec((None, 1, D_MODEL), lambda l, k: (l, 0, k)),
        ],
        out_specs=pl.BlockSpec((None, None, MOD_ROWS, 1, D_MODEL), lambda l, k: (l, k, 0, 0, 0)),
        out_shape=jax.ShapeDtypeStruct((depth, N_MOD, MOD_ROWS, 1, D_MODEL), F32),
        compiler_params=pltpu.CompilerParams(dimension_semantics=("arbitrary", "arbitrary"),
                                             vmem_limit_bytes=_vmem_limit(4 * blk)),
        name="ada_mod",
    )(cs, mod_w, mod_b.reshape(depth, 1, N_MOD * D_MODEL))


def _pw1_glu(x, gain, mods, layer, ctx, w_bf, b):
    bsz, t, _ = x.shape
    tm = _tile_rows(t, PROJ_ROWS)
    est = 4 * _nbytes((tm, D_MODEL), F32) + _nbytes(w_bf.shape, BF16) + 3 * _nbytes((tm, 2 * D_MODEL), F32)
    return pl.pallas_call(
        functools.partial(_pw1_glu_body, tm=tm),
        grid=(bsz, t // tm),
        in_specs=[_row_spec(tm), _vec_spec(), _mod_spec(layer, 0, ctx), _mod_spec(layer, 1, ctx),
                  _const_spec(w_bf.shape), _const_spec(b.shape)],
        out_specs=_row_spec(tm),
        out_shape=jax.ShapeDtypeStruct((bsz, t, D_MODEL), F32),
        compiler_params=_params(est),
        name="pw1_glu",
    )(x, gain, mods, mods, w_bf, b)


def _conv_mix(u, x, w_dw, b_dw, ln_g, ln_b, w2_bf, b2, post_g, mods, layer, ctx):
    bsz, t, _ = x.shape
    tm = _tile_rows(t)
    n_tiles = t // tm
    prev, nxt = _halo_specs(tm, t)
    est = (8 * _nbytes((tm, D_MODEL), F32) + _nbytes(w2_bf.shape, BF16) + _nbytes(w_dw.shape, F32)
           + 4 * _nbytes((tm, D_MODEL), F32))
    return pl.pallas_call(
        functools.partial(_conv_mix_body, tm=tm, n_tiles=n_tiles),
        grid=(bsz, n_tiles),
        in_specs=[prev, _row_spec(tm), nxt, _row_spec(tm),
                  _const_spec(w_dw.shape), _const_spec(b_dw.shape), _vec_spec(), _vec_spec(),
                  _const_spec(w2_bf.shape), _vec_spec(), _vec_spec(), _mod_spec(layer, 2, ctx)],
        out_specs=_row_spec(tm),
        out_shape=jax.ShapeDtypeStruct((bsz, t, D_MODEL), F32),
        scratch_shapes=[pltpu.VMEM((D_MODEL // V7X_LANES, tm + 2 * HALO_ROWS, V7X_LANES), F32),
                        pltpu.VMEM((tm, D_MODEL), F32)],
        compiler_params=_params(est),
        name="conv_mix",
    )(u, u, u, x, w_dw, b_dw, ln_g, ln_b, w2_bf, b2, post_g, mods)


def _layer_spec(stacked_shape, layer):
    nd = len(stacked_shape) - 1
    return pl.BlockSpec((None,) + tuple(stacked_shape[1:]), lambda b, i: (layer,) + (0,) * nd,
                        pipeline_mode=pl.Buffered(1))


def _ffn(x, gain, post_g, mods, layer, ctx, wup, w_dw, b_dw, wd):
    bsz, t, _ = x.shape
    tm = _tile_rows(t, FFN_ROWS)
    n_tiles = t // tm
    prev, nxt = _halo_specs(tm, t)
    est = (8 * _nbytes((tm, D_MODEL), F32) + _nbytes(wup.shape[1:], BF16) + _nbytes(wd.shape[1:], F32) + _nbytes(wd.shape[1:], BF16)
           + _nbytes((tm, D_FF), BF16) + 8 * _nbytes((tm + 2 * HALO_ROWS, FFN_CHUNK), F32))
    return pl.pallas_call(
        functools.partial(_ffn_body, tm=tm, n_tiles=n_tiles),
        grid=(bsz, n_tiles),
        in_specs=[prev, _row_spec(tm), nxt, _vec_spec(), _mod_spec(layer, 3, ctx), _mod_spec(layer, 4, ctx),
                  _layer_spec(wup.shape, layer), _const_spec(w_dw.shape), _const_spec(b_dw.shape),
                  _layer_spec(wd.shape, layer), _vec_spec(), _mod_spec(layer, 5, ctx)],
        out_specs=_row_spec(tm),
        out_shape=jax.ShapeDtypeStruct((bsz, t, D_MODEL), F32),
        compiler_params=_params(est),
        name="conv_ffn",
    )(x, x, x, gain, mods, mods, wup, w_dw, b_dw, wd, post_g, mods)


def _proj(x, gain, mods, layer, ctx, w, col0, out_scales):
    n_out = len(out_scales)
    bsz, t, _ = x.shape
    tm = _tile_rows(t, PROJ_ROWS)
    est = (2 * _nbytes((tm, D_MODEL), F32) + _nbytes(w.shape, F32) + _nbytes(w.shape, BF16)
           + (2 * n_out + 4) * _nbytes((tm, D_MODEL), F32))
    return pl.pallas_call(
        functools.partial(_proj_body, col0=col0, out_scales=tuple(out_scales)),
        grid=(bsz, t // tm),
        in_specs=[_row_spec(tm), _vec_spec(), _mod_spec(layer, 0, ctx), _mod_spec(layer, 1, ctx),
                  _const_spec(w.shape)],
        out_specs=[_row_spec(tm)] * n_out,
        out_shape=[jax.ShapeDtypeStruct((bsz, t, D_MODEL), BF16)] * n_out,
        compiler_params=_params(est),
        name="norm_proj",
    )(x, gain, mods, mods, w)


def _out_proj(a, x, w, post_g, mods, layer):
    bsz, t, _ = x.shape
    tm = _tile_rows(t, PROJ_ROWS)
    est = 6 * _nbytes((tm, D_MODEL), F32) + _nbytes(w.shape, F32) + _nbytes(w.shape, BF16)
    return pl.pallas_call(
        _out_proj_body,
        grid=(bsz, t // tm),
        in_specs=[_row_spec(tm), _row_spec(tm), _const_spec(w.shape), _vec_spec(), _mod_spec(layer, 2, None)],
        out_specs=_row_spec(tm),
        out_shape=jax.ShapeDtypeStruct((bsz, t, D_MODEL), F32),
        compiler_params=_params(est),
        name="out_proj",
    )(a, x, w, post_g, mods)


def _attention(q, k, v, kc, vc, bias):
    bsz, t, _ = q.shape
    rows = t // GRID_W
    c_len = kc.shape[1]
    rps = ATTN_ROWS_PER_STEP
    assert rows % rps == 0
    tq = rps * GRID_W
    est = (4 * _nbytes((t, D_MODEL), BF16) + 4 * _nbytes((c_len, D_MODEL), BF16)
           + _nbytes(bias.shape, F32) + 4 * _nbytes((tq, D_MODEL), BF16))
    return pl.pallas_call(
        functools.partial(_attn_body, rows=rows, rows_per_step=rps),
        grid=(bsz, rows // rps),
        in_specs=[
            pl.BlockSpec((None, tq, D_MODEL), lambda b, r: (b, r, 0)),
            pl.BlockSpec((None, t, D_MODEL), lambda b, r: (b, 0, 0)),
            pl.BlockSpec((None, t, D_MODEL), lambda b, r: (b, 0, 0)),
            pl.BlockSpec((None, c_len, D_MODEL), lambda b, r: (b, 0, 0)),
            pl.BlockSpec((None, c_len, D_MODEL), lambda b, r: (b, 0, 0)),
            _const_spec(bias.shape),
        ],
        out_specs=pl.BlockSpec((None, tq, D_MODEL), lambda b, r: (b, r, 0)),
        out_shape=jax.ShapeDtypeStruct((bsz, t, D_MODEL), BF16),
        compiler_params=_params(est),
        name="na_attention",
    )(q, k, v, kc, vc, bias)


def _attn_bias_table(rpb):
    n_pairs = NA_HEADS // 2
    n_s = 2 * NA_KH - 2
    rp = jnp.pad(rpb.astype(F32), ((0, 0), (0, 0), (0, GRID_W - (2 * NA_KW - 1))))
    src = jnp.concatenate([rp[:, :n_s], rp[:, 1:]], axis=-1)
    src = jnp.transpose(src.reshape(n_pairs, 2, n_s, ATTN_PAIR), (0, 2, 1, 3))[:, :, :, None, :]
    return pl.pallas_call(
        _bias_body,
        grid=(n_pairs,),
        in_specs=[pl.BlockSpec((None, n_s, 2, 1, ATTN_PAIR), lambda p: (p, 0, 0, 0, 0))],
        out_specs=pl.BlockSpec((None, n_s, 2 * GRID_W, ATTN_PAIR), lambda p: (p, 0, 0, 0)),
        out_shape=jax.ShapeDtypeStruct((n_pairs, n_s, 2 * GRID_W, ATTN_PAIR), F32),
        name="na_bias",
    )(src)


def kernel(x, c, ctx, c_ctx, mod_w, mod_b, mix_pre_g, mix_post_g, ffn_pre_g, ffn_post_g, cv_w_pw1, cv_b_pw1, cv_w_dw, cv_b_dw, cv_ln_g, cv_ln_b, cv_w_pw2, cv_b_pw2, na_w_qkv, na_w_o, na_rpb, ffn_w_up, ffn_w_dw, ffn_b_dw, ffn_w_down):
    bsz = x.shape[0]
    assert bsz + 1 <= MOD_ROWS and x.shape[2] == D_MODEL and mod_w.shape[0] == DEPTH
    ctx_row = bsz
    cs = jnp.concatenate([c, c_ctx[None, :], jnp.zeros((MOD_ROWS - bsz - 1, D_MODEL), F32)], axis=0)
    mods = _ada_mod(cs, mod_w, mod_b)
    wup_all = ffn_w_up.astype(BF16)
    wd_all = ffn_w_down

    h_ctx = ctx
    for i in range(DEPTH):
        last = i == DEPTH - 1
        j = i // 2
        use_na = (i % 2) == 1
        pre_g = mix_pre_g[i][None, :]
        post_g = mix_post_g[i][None, :]
        ffn = (wup_all, jnp.broadcast_to(ffn_w_dw[i][:, None, :], (FFN_CONV_WIDTH, V7X_SUBLANES, D_FF)),
               jnp.broadcast_to(ffn_b_dw[i][None, :], (V7X_SUBLANES, D_FF)), wd_all)
        if use_na:
            q, k, v = _proj(x, pre_g, mods, i, None, na_w_qkv[j], 0, (NA_HEAD_DIM ** -0.5, 1.0, 1.0))
            kc, vc = _proj(h_ctx, pre_g, mods, i, ctx_row, na_w_qkv[j], D_MODEL, (1.0, 1.0))
            attn = _attention(q, k, v, kc, vc, _attn_bias_table(na_rpb[j]))
            x = _out_proj(attn, x, na_w_o[j], post_g, mods, i)
            assert last, "context queries (an attention layer that is not the last) are not implemented"
        else:
            w1 = cv_w_pw1[j].astype(BF16)
            b1 = cv_b_pw1[j][None, :]
            w_dw = jnp.broadcast_to(cv_w_dw[j][:, None, :], (CONV_WIDTH, V7X_SUBLANES, D_MODEL))
            b_dw = jnp.broadcast_to(cv_b_dw[j][None, :], (V7X_SUBLANES, D_MODEL))
            w2 = cv_w_pw2[j].astype(BF16)
            cv = (w_dw, b_dw, cv_ln_g[j][None, :], cv_ln_b[j][None, :], w2, cv_b_pw2[j][None, :], post_g)
            u = _pw1_glu(x, pre_g, mods, i, None, w1, b1)
            x = _conv_mix(u, x, *cv, mods, i, None)
            if not last:
                u_ctx = _pw1_glu(h_ctx, pre_g, mods, i, ctx_row, w1, b1)
                h_ctx = _conv_mix(u_ctx, h_ctx, *cv, mods, i, ctx_row)
        x = _ffn(x, ffn_pre_g[i][None, :], ffn_post_g[i][None, :], mods, i, None, *ffn)
        if not last:
            h_ctx = _ffn(h_ctx, ffn_pre_g[i][None, :], ffn_post_g[i][None, :], mods, i, ctx_row, *ffn)
    return x
```
